```python
import math
import jax, jax.numpy as jnp
from jax import lax
import numpy as np

D_MODEL = 1024
BATCH = 16
SEQ = 2048
DEPTH = 4

MEM_LEN = 256
ML_HEADS = 4
ML_DHEAD = 128
SB_HEADS = 4
SB_DHEAD = 128
ML_WIDTH = ML_HEADS * ML_DHEAD
SB_WIDTH = SB_HEADS * SB_DHEAD
D_MIX = ML_WIDTH + SB_WIDTH
CONV_W = 4
ML_CHUNK = 128
SB_BLOCK = 128
IN_SIZES = (ML_WIDTH, ML_WIDTH, ML_WIDTH, ML_WIDTH, ML_HEADS, ML_HEADS, SB_WIDTH, SB_WIDTH, SB_WIDTH)
D_IN = 4 * ML_WIDTH + 2 * ML_HEADS + 3 * SB_WIDTH
XA_HEADS = 4
XA_DHEAD = D_MODEL // XA_HEADS
N_GROUPS = 4
EXP_PER_GROUP = 4
TOP_K = 2
D_EXPERT = 512
EPS = 1e-6

kernel_name = 'hybrid_mlstm_stickbreak_hmoe'


def rmsnorm(x, g):
    x32 = x.astype(jnp.float32)
    y = x32 * lax.rsqrt(jnp.mean(x32 * x32, axis=-1, keepdims=True) + EPS)
    return y.astype(x.dtype) * g


def head_rmsnorm(y, g):
    H, d = y.shape[-2], y.shape[-1]
    y32 = y.astype(jnp.float32)
    y32 = y32 * lax.rsqrt(jnp.mean(y32 * y32, axis=-1, keepdims=True) + EPS)
    return y32.astype(g.dtype) * g.reshape(H, d)


def causal_conv(x, w, b):
    S = x.shape[1]
    xp = jnp.pad(x, ((0, 0), (CONV_W - 1, 0), (0, 0)))
    y = b
    for tap in range(CONV_W):
        y = y + xp[:, tap:tap + S] * w[tap]
    return y


def mlstm_chunkwise(q, k, v, i_pre, f_pre):
    B, S, H, d = q.shape
    L = ML_CHUNK
    nc = S // L
    f32 = jnp.float32

    def chunks(t):
        return t.astype(f32).reshape(B, nc, L, H, d).transpose(1, 0, 3, 2, 4)

    def gchunks(t):
        return t.astype(f32).reshape(B, nc, L, H).transpose(1, 0, 3, 2)

    qc = chunks(q)
    kc = chunks(k) * (d ** -0.5)
    vc = chunks(v)
    ic = gchunks(i_pre)
    bc = jnp.cumsum(jax.nn.log_sigmoid(gchunks(f_pre)), axis=-1)
    causal = jnp.tril(jnp.ones((L, L), dtype=bool))

    def step(carry, inp):
        C, n, m = carry
        q_, k_, v_, i_, b_ = inp
        logd = b_[..., :, None] - b_[..., None, :] + i_[..., None, :]
        logd = jnp.where(causal, logd, -jnp.inf)
        inter = b_ + m[..., None]
        m_t = jnp.maximum(inter, jnp.max(logd, axis=-1))
        s = jnp.einsum('bhtd,bhsd->bhts', q_, k_) * jnp.exp(logd - m_t[..., None])
        w_inter = jnp.exp(inter - m_t)
        num = jnp.einsum('bhts,bhsd->bhtd', s, v_) + w_inter[..., None] * jnp.einsum('bhvk,bhtk->bhtv', C, q_)
        den = jnp.sum(s, axis=-1) + w_inter * jnp.einsum('bhk,bhtk->bht', n, q_)
        h = num / jnp.maximum(jnp.abs(den), jnp.exp(-m_t))[..., None]
        b_last = b_[..., -1]
        end_log = b_last[..., None] - b_ + i_
        m_new = jnp.maximum(b_last + m, jnp.max(end_log, axis=-1))
        w_old = jnp.exp(b_last + m - m_new)
        w_s = jnp.exp(end_log - m_new[..., None])
        C_new = w_old[..., None, None] * C + jnp.einsum('bhs,bhsv,bhsk->bhvk', w_s, v_, k_)
        n_new = w_old[..., None] * n + jnp.einsum('bhs,bhsk->bhk', w_s, k_)
        return (C_new, n_new, m_new), h

    init = (jnp.zeros((B, H, d, d), f32), jnp.zeros((B, H, d), f32), jnp.zeros((B, H), f32))
    _, hc = lax.scan(step, init, (qc, kc, vc, ic, bc))
    return hc.transpose(1, 0, 3, 2, 4).reshape(B, S, H, d)


def stick_breaking(q, k, v):
    B, S, H, d = q.shape
    f32 = jnp.float32
    qh = q.astype(f32).transpose(0, 2, 1, 3) * (d ** -0.5)
    kh = k.astype(f32).transpose(0, 2, 1, 3)
    vh = v.astype(f32).transpose(0, 2, 1, 3)
    outs = []
    for blk in range(S // SB_BLOCK):
        t0 = blk * SB_BLOCK
        t1 = t0 + SB_BLOCK
        z = jnp.einsum('bhtd,bhsd->bhts', qh[:, :, t0:t1], kh[:, :, :t1])
        t_idx = t0 + jnp.arange(SB_BLOCK)[:, None]
        s_idx = jnp.arange(t1)[None, :]
        mask = s_idx < t_idx
        log_1m = jnp.where(mask, jax.nn.log_sigmoid(-z), 0.0)
        tail = lax.cumsum(log_1m, axis=3, reverse=True) - log_1m
        a = jnp.where(mask, jnp.exp(jax.nn.log_sigmoid(z) + tail), 0.0)
        outs.append(jnp.einsum('bhts,bhsd->bhtd', a, vh[:, :, :t1]))
    o = jnp.concatenate(outs, axis=2)
    return o.transpose(0, 2, 1, 3)


def parallel_mixer(h, w_in, conv_w, conv_b, i_bias, f_bias, ml_head_g, sb_head_g, w_out):
    B, S, _ = h.shape
    proj = h @ w_in
    cuts = []
    acc = 0
    for sz in IN_SIZES[:-1]:
        acc += sz
        cuts.append(acc)
    mq, mk, mv, mo, mi, mf, sq, sk, sv = jnp.split(proj, cuts, axis=-1)
    qk = jax.nn.silu(causal_conv(jnp.concatenate([mq, mk], axis=-1), conv_w, conv_b))
    mq, mk = qk[..., :ML_WIDTH], qk[..., ML_WIDTH:]
    hs = (B, S, ML_HEADS, ML_DHEAD)
    h_ml = mlstm_chunkwise(mq.reshape(hs), mk.reshape(hs), mv.reshape(hs), mi + i_bias, mf + f_bias)
    h_ml = jax.nn.sigmoid(mo).reshape(hs) * head_rmsnorm(h_ml.astype(h.dtype), ml_head_g)
    ss = (B, S, SB_HEADS, SB_DHEAD)
    h_sb = stick_breaking(sq.reshape(ss), sk.reshape(ss), sv.reshape(ss))
    h_sb = head_rmsnorm(h_sb.astype(h.dtype), sb_head_g)
    y = jnp.concatenate([h_ml.reshape(B, S, ML_WIDTH), h_sb.reshape(B, S, SB_WIDTH)], axis=-1)
    return y @ w_out


def memory_attention(h, m, w_mq, w_mk, w_mv, w_mo):
    B, S, D = h.shape
    M = m.shape[1]
    q = (h @ w_mq).reshape(B, S, XA_HEADS, XA_DHEAD)
    k = (m @ w_mk).reshape(B, M, XA_HEADS, XA_DHEAD)
    v = (m @ w_mv).reshape(B, M, XA_HEADS, XA_DHEAD)
    s = jnp.einsum('bshd,bmhd->bhsm', q, k).astype(jnp.float32) * (XA_DHEAD ** -0.5)
    p = jax.nn.softmax(s, axis=-1).astype(v.dtype)
    o = jnp.einsum('bhsm,bmhd->bshd', p, v).reshape(B, S, D)
    return o @ w_mo


def hier_moe(h, w_rg, b_rg, w_re, b_re, w_e1, w_e3, w_e2):
    B, S, D = h.shape
    t = h.reshape(-1, D)
    n_tok = t.shape[0]
    p_g = jax.nn.softmax((t @ w_rg + b_rg).astype(jnp.float32), axis=-1)
    onehot_g = jax.nn.one_hot(jnp.argmax(p_g, axis=-1), N_GROUPS, dtype=jnp.float32)
    p_sel = jnp.sum(p_g * onehot_g, axis=-1)
    e_logits = (t @ w_re + b_re).astype(jnp.float32).reshape(n_tok, N_GROUPS, EXP_PER_GROUP)
    e_sel = jnp.einsum('nge,ng->ne', e_logits, onehot_g)
    top_w, top_i = lax.top_k(jax.nn.softmax(e_sel, axis=-1), TOP_K)
    top_w = top_w / jnp.sum(top_w, axis=-1, keepdims=True)
    e_gate = jnp.sum(jax.nn.one_hot(top_i, EXP_PER_GROUP, dtype=jnp.float32) * top_w[..., None], axis=1)
    gate = (onehot_g[:, :, None] * (p_sel[:, None] * e_gate)[:, None, :]).astype(h.dtype)
    y = jnp.zeros((n_tok, D), h.dtype)
    for g in range(N_GROUPS):
        a = jnp.einsum('nd,edf->nef', t, w_e1[g])
        u = jnp.einsum('nd,edf->nef', t, w_e3[g])
        y = y + jnp.einsum('nef,efd->nd', jax.nn.silu(a) * u * gate[:, g, :, None], w_e2[g])
    return y.reshape(B, S, D)


def setup_inputs(seed: int = 0) -> dict:
    key = jax.random.key(seed)
    ks = jax.random.split(key, 32)
    nrm = jax.random.normal
    f32 = jnp.float32
    G, E, F = N_GROUPS, EXP_PER_GROUP, D_EXPERT
    return {
        'x': nrm(ks[0], (BATCH, SEQ, D_MODEL), f32),
        'mem': nrm(ks[1], (BATCH, MEM_LEN, D_MODEL), f32),
        'ln_mix': 1.0 + 0.02 * nrm(ks[2], (DEPTH, D_MODEL), f32),
        'w_in': nrm(ks[3], (DEPTH, D_MODEL, D_IN), f32) * D_MODEL ** -0.5,
        'conv_w': nrm(ks[4], (DEPTH, CONV_W, 2 * ML_WIDTH), f32) * CONV_W ** -0.5,
        'conv_b': 0.02 * nrm(ks[5], (DEPTH, 2 * ML_WIDTH), f32),
        'i_bias': 0.1 * nrm(ks[6], (DEPTH, ML_HEADS), f32),
        'f_bias': jnp.linspace(3.0, 6.0, ML_HEADS, dtype=f32)[None, :] + 0.1 * nrm(ks[7], (DEPTH, ML_HEADS), f32),
        'ml_head_g': 1.0 + 0.02 * nrm(ks[8], (DEPTH, ML_WIDTH), f32),
        'sb_head_g': 1.0 + 0.02 * nrm(ks[9], (DEPTH, SB_WIDTH), f32),
        'w_out': nrm(ks[10], (DEPTH, D_MIX, D_MODEL), f32) * D_MIX ** -0.5,
        'ln_mem': 1.0 + 0.02 * nrm(ks[11], (DEPTH, D_MODEL), f32),
        'ln_memkv': 1.0 + 0.02 * nrm(ks[12], (DEPTH, D_MODEL), f32),
        'w_mq': nrm(ks[13], (DEPTH, D_MODEL, D_MODEL), f32) * D_MODEL ** -0.5,
        'w_mk': nrm(ks[14], (DEPTH, D_MODEL, D_MODEL), f32) * D_MODEL ** -0.5,
        'w_mv': nrm(ks[15], (DEPTH, D_MODEL, D_MODEL), f32) * D_MODEL ** -0.5,
        'w_mo': nrm(ks[16], (DEPTH, D_MODEL, D_MODEL), f32) * D_MODEL ** -0.5,
        'ln_ffn': 1.0 + 0.02 * nrm(ks[17], (DEPTH, D_MODEL), f32),
        'w_rg': nrm(ks[18], (DEPTH, D_MODEL, G), f32) * D_MODEL ** -0.5,
        'b_rg': 0.01 * nrm(ks[19], (DEPTH, G), f32),
        'w_re': nrm(ks[20], (DEPTH, D_MODEL, G * E), f32) * D_MODEL ** -0.5,
        'b_re': 0.01 * nrm(ks[21], (DEPTH, G * E), f32),
        'w_e1': nrm(ks[22], (DEPTH, G, E, D_MODEL, F), f32) * D_MODEL ** -0.5,
        'w_e3': nrm(ks[23], (DEPTH, G, E, D_MODEL, F), f32) * D_MODEL ** -0.5,
        'w_e2': nrm(ks[24], (DEPTH, G, E, F, D_MODEL), f32) * F ** -0.5,
        'ln_final': 1.0 + 0.02 * nrm(ks[25], (D_MODEL,), f32),
    }


def reference(x, mem, ln_mix, w_in, conv_w, conv_b, i_bias, f_bias, ml_head_g, sb_head_g, w_out,
              ln_mem, ln_memkv, w_mq, w_mk, w_mv, w_mo, ln_ffn, w_rg, b_rg, w_re, b_re,
              w_e1, w_e3, w_e2, ln_final):
    for l in range(DEPTH):
        h = rmsnorm(x, ln_mix[l])
        x = x + parallel_mixer(h, w_in[l], conv_w[l], conv_b[l], i_bias[l], f_bias[l],
                               ml_head_g[l], sb_head_g[l], w_out[l])
        h = rmsnorm(x, ln_mem[l])
        m = rmsnorm(mem, ln_memkv[l])
        x = x + memory_attention(h, m, w_mq[l], w_mk[l], w_mv[l], w_mo[l])
        h = rmsnorm(x, ln_ffn[l])
        x = x + hier_moe(h, w_rg[l], b_rg[l], w_re[l], b_re[l], w_e1[l], w_e3[l], w_e2[l])
    return rmsnorm(x, ln_final)
```

```python
import functools

import jax
import jax.numpy as jnp
from jax import lax
from jax.experimental import pallas as pl
from jax.experimental.pallas import tpu as pltpu

F32 = jnp.float32
BF16 = jnp.bfloat16
I32 = jnp.int32

ML_HEADS = 4
ML_DHEAD = 128
SB_HEADS = 4
SB_DHEAD = 128
ML_WIDTH = ML_HEADS * ML_DHEAD
SB_WIDTH = SB_HEADS * SB_DHEAD
CONV_W = 4
ML_CHUNK = 128
SB_BLOCK = 128
XA_HEADS = 4
N_GROUPS = 4
EXP_PER_GROUP = 4
EPS = 1e-6

LANES = 128
ROW_TILE_BF16 = 16
SORT_BLOCK = 256
SORT_WIN = 384
MOE_CHUNK = 128
EXPERTS_PER_STEP = 2
TOKEN_TILE = 512
VMEM_LIMIT = 56 * 1024 * 1024


def _dot(a, b):
    return jnp.dot(a, b, preferred_element_type=F32)


def _dot_nt(a, b):
    return lax.dot_general(a, b, (((1,), (1,)), ((), ())), preferred_element_type=F32)


def _split(x):
    hi = x.astype(BF16)
    lo = (x - hi.astype(F32)).astype(BF16)
    return hi, lo


def _rms(x, g):
    return x * lax.rsqrt(jnp.mean(x * x, axis=-1, keepdims=True) + EPS) * g


def _sigmoid(x):
    return 1.0 / (1.0 + jnp.exp(-x))


def _softplus(x):
    return jnp.maximum(x, 0.0) + jnp.log1p(jnp.exp(-jnp.abs(x)))


def _params(sem):
    return pltpu.CompilerParams(dimension_semantics=sem, vmem_limit_bytes=VMEM_LIMIT)


def _memkv_kernel(mem_ref, g_ref, wk_ref, wv_ref, k_ref, v_ref):
    h = _rms(mem_ref[0], g_ref[0]).astype(BF16)
    k_ref[0, 0] = _dot(h, wk_ref[0]).astype(BF16)
    v_ref[0, 0] = _dot(h, wv_ref[0]).astype(BF16)


def _memkv(mem, ln_memkv, w_mk, w_mv):
    B, M, D = mem.shape
    depth = w_mk.shape[0]
    return pl.pallas_call(
        _memkv_kernel,
        grid=(depth, B),
        in_specs=[
            pl.BlockSpec((1, M, D), lambda l, b: (b, 0, 0)),
            pl.BlockSpec((1, 1, D), lambda l, b: (l, 0, 0)),
            pl.BlockSpec((1, D, D), lambda l, b: (l, 0, 0)),
            pl.BlockSpec((1, D, D), lambda l, b: (l, 0, 0)),
        ],
        out_specs=[
            pl.BlockSpec((1, 1, M, D), lambda l, b: (l, b, 0, 0)),
            pl.BlockSpec((1, 1, M, D), lambda l, b: (l, b, 0, 0)),
        ],
        out_shape=[jax.ShapeDtypeStruct((depth, B, M, D), BF16)] * 2,
        compiler_params=_params(("arbitrary", "arbitrary")),
        name="memkv",
    )(mem, ln_memkv.reshape(depth, 1, D), w_mk, w_mv)


def _proj_kernel(has_y, *refs):
    if has_y:
        x_ref, y_ref = refs[:2]
        refs = refs[2:]
        x = x_ref[...] + y_ref[...].astype(F32)
    else:
        x_ref = refs[0]
        refs = refs[1:]
        x = x_ref[...]
    g_ref, wml_ref, wsb_ref, wg_ref, wgt_ref, bg_ref, bgt_ref, ml_ref, sb_ref, gc_ref, gt_ref = refs
    h = _rms(x, g_ref[0]).astype(BF16)
    ml_ref[...] = _dot(h, wml_ref[0]).astype(BF16)
    sb_ref[...] = _dot(h, wsb_ref[0]).astype(BF16)
    gc_ref[...] = _dot(h, wg_ref[0]) + bg_ref[0]
    gt_ref[...] = _dot_nt(wgt_ref[0], h) + bgt_ref[0]


def _proj(l, x2, y, ln_mix, w_ml, w_sb, w_g, w_gt, b_g, b_gt):
    N, D = x2.shape
    tm = TOKEN_TILE
    row = lambda i: (i, 0)
    lay = lambda i: (l, 0, 0)
    ins = [x2] + ([y] if y is not None else [])
    in_specs = [pl.BlockSpec((tm, D), row)] * len(ins) + [
        pl.BlockSpec((1, 1, D), lay),
        pl.BlockSpec((1, D, 4 * ML_WIDTH), lay),
        pl.BlockSpec((1, D, 3 * SB_WIDTH), lay),
        pl.BlockSpec((1, D, LANES), lay),
        pl.BlockSpec((1, 8, D), lay),
        pl.BlockSpec((1, 1, LANES), lay),
        pl.BlockSpec((1, 8, 1), lay),
    ]
    return pl.pallas_call(
        functools.partial(_proj_kernel, y is not None),
        grid=(N // tm,),
        in_specs=in_specs,
        out_specs=[
            pl.BlockSpec((tm, 4 * ML_WIDTH), row),
            pl.BlockSpec((tm, 3 * SB_WIDTH), row),
            pl.BlockSpec((tm, LANES), row),
            pl.BlockSpec((8, tm), lambda i: (0, i)),
        ],
        out_shape=[
            jax.ShapeDtypeStruct((N, 4 * ML_WIDTH), BF16),
            jax.ShapeDtypeStruct((N, 3 * SB_WIDTH), BF16),
            jax.ShapeDtypeStruct((N, LANES), F32),
            jax.ShapeDtypeStruct((8, N), F32),
        ],
        compiler_params=_params(("arbitrary",)),
        name="proj",
    )(*ins, ln_mix, w_ml, w_sb, w_g, w_gt, b_g, b_gt)


def _mlstm_kernel(ml_ref, gc_ref, gt_ref, cw_ref, cb_ref, hg_ref, out_ref, ct_ref, n_ref, m_ref, prev_ref):
    L, d, W, H = ML_CHUNK, ML_DHEAD, ML_WIDTH, ML_HEADS

    @pl.when(pl.program_id(1) == 0)
    def _():
        ct_ref[...] = jnp.zeros_like(ct_ref)
        n_ref[...] = jnp.zeros_like(n_ref)
        m_ref[...] = jnp.zeros_like(m_ref)
        prev_ref[...] = jnp.zeros_like(prev_ref)

    qk = ml_ref[:, : 2 * W].astype(F32)
    ext = jnp.concatenate([prev_ref[...], qk], axis=0)
    prev_ref[...] = qk[L - 8:, :]
    cw = cw_ref[0]
    acc = cb_ref[0]
    for tap in range(CONV_W):
        lo = 8 - (CONV_W - 1) + tap
        acc = acc + ext[lo:lo + L, :] * cw[tap:tap + 1, :]
    qa = acc * _sigmoid(acc)

    gt = gt_ref[...]
    gc = gc_ref[...]
    row = lax.broadcasted_iota(I32, (L, L), 0)
    col = lax.broadcasted_iota(I32, (L, L), 1)
    causal = col <= row
    tri = jnp.where(causal, 1.0, 0.0).astype(BF16)
    tri_t = jnp.where(row <= col, 1.0, 0.0).astype(BF16)
    lsc_hi, lsc_lo = _split(-_softplus(-gc))
    bc = _dot(tri, lsc_hi) + _dot(tri, lsc_lo)
    lst_hi, lst_lo = _split(-_softplus(-gt))
    bt = _dot(lst_hi, tri_t) + _dot(lst_lo, tri_t)
    hg = hg_ref[0]

    for h in range(H):
        q = qa[:, h * d:(h + 1) * d]
        k = qa[:, W + h * d:W + (h + 1) * d] * (d ** -0.5)
        v = ml_ref[:, 2 * W + h * d:2 * W + (h + 1) * d]
        og = ml_ref[:, 3 * W + h * d:3 * W + (h + 1) * d].astype(F32)
        i_r = gt[h:h + 1, :]
        b_r = bt[H + h:H + h + 1, :]
        i_c = gc[:, h:h + 1]
        b_c = bc[:, H + h:H + h + 1]
        m_prev = m_ref[h][:1, :1]
        logd = jnp.where(causal, b_c - b_r + i_r, -jnp.inf)
        inter = b_c + m_prev
        m_t = jnp.maximum(inter, jnp.max(logd, axis=-1, keepdims=True))
        qb = q.astype(BF16)
        s = _dot_nt(qb, k.astype(BF16)) * jnp.exp(logd - m_t)
        w_inter = jnp.exp(inter - m_t)
        ct = ct_ref[h]
        n_row = n_ref[h][:1, :]
        num = _dot(s.astype(BF16), v) + w_inter * _dot(qb, ct.astype(BF16))
        den = jnp.sum(s, axis=-1, keepdims=True) + w_inter * jnp.sum(q * n_row, axis=-1, keepdims=True)
        hh = num / jnp.maximum(jnp.abs(den), jnp.exp(-m_t))

        b_last = b_r[:, L - 1:L]
        m_new = jnp.maximum(b_last + m_prev, jnp.max(b_last - b_r + i_r, axis=-1, keepdims=True))
        w_old = jnp.exp(b_last + m_prev - m_new)
        kw = k * jnp.exp(b_last - b_c + i_c - m_new)
        ct_ref[h] = w_old * ct + _dot(kw.T.astype(BF16), v)
        n_ref[h] = jnp.broadcast_to(w_old * n_row + jnp.sum(kw, axis=0, keepdims=True), (8, d))
        m_ref[h] = jnp.broadcast_to(m_new, (8, LANES))

        hn = hh * lax.rsqrt(jnp.mean(hh * hh, axis=-1, keepdims=True) + EPS) * hg[:, h * d:(h + 1) * d]
        out_ref[:, h * d:(h + 1) * d] = (_sigmoid(og) * hn).astype(BF16)


def _mlstm(l, B, S, ml, gc, gt, conv_w, conv_b, ml_head_g):
    L, W, H, d = ML_CHUNK, ML_WIDTH, ML_HEADS, ML_DHEAD
    nc = S // L
    lay = lambda b, c: (l, 0, 0)
    return pl.pallas_call(
        _mlstm_kernel,
        grid=(B, nc),
        in_specs=[
            pl.BlockSpec((L, 4 * W), lambda b, c: (b * nc + c, 0)),
            pl.BlockSpec((L, LANES), lambda b, c: (b * nc + c, 0)),
            pl.BlockSpec((8, L), lambda b, c: (0, b * nc + c)),
            pl.BlockSpec((1, CONV_W, 2 * W), lay),
            pl.BlockSpec((1, 1, 2 * W), lay),
            pl.BlockSpec((1, 1, W), lay),
        ],
        out_specs=pl.BlockSpec((L, W), lambda b, c: (b * nc + c, 0)),
        out_shape=jax.ShapeDtypeStruct((B * S, W), BF16),
        scratch_shapes=[
            pltpu.VMEM((H, d, d), F32),
            pltpu.VMEM((H, 8, d), F32),
            pltpu.VMEM((H, 8, LANES), F32),
            pltpu.VMEM((8, 2 * W), F32),
        ],
        compiler_params=_params(("arbitrary", "arbitrary")),
        name="mlstm",
    )(ml, gc, gt, conv_w, conv_b, ml_head_g)


def _sb_kernel(q_ref, k_ref, v_ref, g_ref, o_ref):
    Lb, d = SB_BLOCK, SB_DHEAD
    i = pl.program_id(2)
    q = q_ref[...]
    scale = d ** -0.5
    row = lax.broadcasted_iota(I32, (Lb, Lb), 0)
    col = lax.broadcasted_iota(I32, (Lb, Lb), 1)
    strict = col < row
    after = jnp.where(row > col, 1.0, 0.0).astype(BF16)

    def block(j):
        start = pl.multiple_of(j * Lb, Lb)
        z = _dot_nt(q, k_ref[pl.ds(start, Lb), :]) * scale
        sp = _softplus(z)
        return -sp, z - sp, v_ref[pl.ds(start, Lb), :]

    def tail_of(l1m):
        hi, lo = _split(l1m)
        return _dot(hi, after) + _dot(lo, after)

    l1m, lsig, vj = block(i)
    l1m = jnp.where(strict, l1m, 0.0)
    a = jnp.where(strict, jnp.exp(lsig + tail_of(l1m)), 0.0)
    acc = _dot(a.astype(BF16), vj)
    rsum = jnp.sum(l1m, axis=-1, keepdims=True)

    def body(jj, carry):
        acc, rsum = carry
        l1m, lsig, vj = block(i - 1 - jj)
        a = jnp.exp(lsig + tail_of(l1m) + rsum)
        return acc + _dot(a.astype(BF16), vj), rsum + jnp.sum(l1m, axis=-1, keepdims=True)

    acc, _ = lax.fori_loop(0, i, body, (acc, rsum))
    o = acc * lax.rsqrt(jnp.mean(acc * acc, axis=-1, keepdims=True) + EPS) * g_ref[0]
    o_ref[...] = o.astype(BF16)


def _stickbreak(l, B, S, sb, sb_head_g):
    Lb, H, d = SB_BLOCK, SB_HEADS, SB_DHEAD
    nq = S // Lb
    return pl.pallas_call(
        _sb_kernel,
        grid=(B, H, nq),
        in_specs=[
            pl.BlockSpec((Lb, d), lambda b, h, i: (b * nq + i, h)),
            pl.BlockSpec((S, d), lambda b, h, i: (b, H + h)),
            pl.BlockSpec((S, d), lambda b, h, i: (b, 2 * H + h)),
            pl.BlockSpec((1, 1, d), lambda b, h, i: (l, 0, h)),
        ],
        out_specs=pl.BlockSpec((Lb, d), lambda b, h, i: (b * nq + i, h)),
        out_shape=jax.ShapeDtypeStruct((B * S, H * d), BF16),
        compiler_params=_params(("arbitrary", "arbitrary", "arbitrary")),
        name="stickbreak",
    )(sb, sb, sb, sb_head_g)


def _route(lgt, tm):
    G, E = N_GROUPS, EXP_PER_GROUP
    lg = [lgt[i:i + 1, :] for i in range(G + G * E)]
    mg = functools.reduce(jnp.maximum, lg[:G])
    eg = [jnp.exp(v - mg) for v in lg[:G]]
    zg = functools.reduce(jnp.add, eg)
    pg = [v / zg for v in eg]
    p_sel = functools.reduce(jnp.maximum, pg)
    gid = jnp.where(pg[0] == p_sel, 0, jnp.where(pg[1] == p_sel, 1, jnp.where(pg[2] == p_sel, 2, 3)))
    es = [jnp.where(gid == 0, lg[G + e], jnp.where(gid == 1, lg[G + E + e],
          jnp.where(gid == 2, lg[G + 2 * E + e], lg[G + 3 * E + e]))) for e in range(E)]
    me = functools.reduce(jnp.maximum, es)
    ee = [jnp.exp(v - me) for v in es]
    ze = functools.reduce(jnp.add, ee)
    pe = [v / ze for v in ee]

    def first_max(vals):
        top = functools.reduce(jnp.maximum, vals)
        idx = jnp.where(vals[0] == top, 0, jnp.where(vals[1] == top, 1, jnp.where(vals[2] == top, 2, 3)))
        return top, idx

    v1, i1 = first_max(pe)
    v2, i2 = first_max([jnp.where(i1 == e, -1.0, pe[e]) for e in range(E)])
    tsum = v1 + v2
    tw1, tw2 = v1 / tsum, v2 / tsum
    gates = [p_sel * (jnp.where(i1 == e, tw1, 0.0) + jnp.where(i2 == e, tw2, 0.0)) for e in range(E)]

    sub8 = lax.broadcasted_iota(I32, (8, tm), 0)
    onehot = jnp.where(sub8 == gid, 1.0, 0.0)
    r = lax.broadcasted_iota(I32, (SORT_BLOCK, SORT_BLOCK), 0)
    c = lax.broadcasted_iota(I32, (SORT_BLOCK, SORT_BLOCK), 1)
    before = jnp.where(r < c, 1.0, 0.0).astype(BF16)
    lane = lax.broadcasted_iota(I32, (8, LANES), 1)
    counts = jnp.zeros((8, LANES), F32)
    lps = []
    for jj in range(tm // SORT_BLOCK):
        oj = onehot[:, jj * SORT_BLOCK:(jj + 1) * SORT_BLOCK]
        pre = _dot(oj.astype(BF16), before)
        cnt = jnp.sum(oj, axis=-1, keepdims=True)
        pcnt = jnp.floor((cnt + (ROW_TILE_BF16 - 1.0)) * (1.0 / ROW_TILE_BF16)) * ROW_TILE_BF16
        start = jnp.zeros((1, 1), F32)
        lp = jnp.zeros((1, SORT_BLOCK), F32)
        for g in range(G):
            lp = lp + oj[g:g + 1, :] * (pre[g:g + 1, :] + start)
            start = start + pcnt[g:g + 1, :]
        lps.append(lp)
        counts = counts + jnp.where(lane == jj, cnt, 0.0)
    lp = jnp.concatenate(lps, axis=1)
    rows = gates + [gid.astype(F32), lp]
    pack = jnp.zeros((8, tm), F32)
    for kk, v in enumerate(rows):
        pack = pack + jnp.where(sub8 == kk, v, 0.0)
    return pack, counts.astype(I32)


def _post_kernel(has_y, *refs):
    if has_y:
        x_ref, y_ref = refs[:2]
        refs = refs[2:]
        x = x_ref[...] + y_ref[...].astype(F32)
    else:
        x_ref = refs[0]
        refs = refs[1:]
        x = x_ref[...]
    (hml_ref, hsb_ref, wo_ref, gmem_ref, wq_ref, k_ref, v_ref, wmo_ref, gffn_ref, wrh_ref, wrl_ref, br_ref,
     x2_ref, h3_ref, rt_ref, gcol_ref, cnt_ref) = refs
    tm = x.shape[0]
    x1 = x + _dot(hml_ref[...], wo_ref[0, :ML_WIDTH, :]) + _dot(hsb_ref[...], wo_ref[0, ML_WIDTH:, :])
    q = _dot(_rms(x1, gmem_ref[0]).astype(BF16), wq_ref[0]).astype(BF16)
    kk = k_ref[0, 0]
    vv = v_ref[0, 0]
    dh = q.shape[1] // XA_HEADS
    outs = []
    for hd in range(XA_HEADS):
        sl = slice(hd * dh, (hd + 1) * dh)
        s = _dot_nt(q[:, sl], kk[:, sl]) * (dh ** -0.5)
        p = jnp.exp(s - jnp.max(s, axis=-1, keepdims=True))
        p = p / jnp.sum(p, axis=-1, keepdims=True)
        outs.append(_dot(p.astype(BF16), vv[:, sl]).astype(BF16))
    x2 = x1 + _dot(jnp.concatenate(outs, axis=-1), wmo_ref[0])
    x2_ref[...] = x2
    h3_hi, h3_lo = _split(_rms(x2, gffn_ref[0]))
    h3_ref[...] = h3_hi
    wrh = wrh_ref[0]
    lg = _dot(h3_hi, wrh) + _dot(h3_lo, wrh) + _dot(h3_hi, wrl_ref[0]) + br_ref[0]
    pack, counts = _route(lg.T, tm)
    rt_ref[...] = pack
    gcol_ref[...] = jnp.concatenate([pack, jnp.zeros((LANES - 8, tm), F32)], axis=0).T
    cnt_ref[0] = counts


def _post(l, B, S, x2, y, hml, hsb, w_out, ln_mem, w_mq, kx, vx, w_mo, ln_ffn, wr_hi, wr_lo, b_r):
    N, D = x2.shape
    M = kx.shape[2]
    tm = TOKEN_TILE
    tpb = S // tm
    row = lambda i: (i, 0)
    lay = lambda i: (l, 0, 0)
    ins = [x2] + ([y] if y is not None else [])
    in_specs = [pl.BlockSpec((tm, D), row)] * len(ins) + [
        pl.BlockSpec((tm, ML_WIDTH), row),
        pl.BlockSpec((tm, SB_WIDTH), row),
        pl.BlockSpec((1, D, D), lay),
        pl.BlockSpec((1, 1, D), lay),
        pl.BlockSpec((1, D, D), lay),
        pl.BlockSpec((1, 1, M, D), lambda i: (l, i // tpb, 0, 0)),
        pl.BlockSpec((1, 1, M, D), lambda i: (l, i // tpb, 0, 0)),
        pl.BlockSpec((1, D, D), lay),
        pl.BlockSpec((1, 1, D), lay),
        pl.BlockSpec((1, D, LANES), lay),
        pl.BlockSpec((1, D, LANES), lay),
        pl.BlockSpec((1, 1, LANES), lay),
    ]
    return pl.pallas_call(
        functools.partial(_post_kernel, y is not None),
        grid=(N // tm,),
        in_specs=in_specs,
        out_specs=[
            pl.BlockSpec((tm, D), row),
            pl.BlockSpec((tm, D), row),
            pl.BlockSpec((8, tm), lambda i: (0, i)),
            pl.BlockSpec((tm, LANES), row),
            pl.BlockSpec((1, 8, LANES), lambda i: (i, 0, 0)),
        ],
        out_shape=[
            jax.ShapeDtypeStruct((N, D), F32),
            jax.ShapeDtypeStruct((N, D), BF16),
            jax.ShapeDtypeStruct((8, N), F32),
            jax.ShapeDtypeStruct((N, LANES), F32),
            jax.ShapeDtypeStruct((N // tm, 8, LANES), I32),
        ],
        compiler_params=_params(("arbitrary",)),
        name="post",
    )(*ins, hml, hsb, w_out, ln_mem, w_mq, kx, vx, w_mo, ln_ffn, wr_hi, wr_lo, b_r)


def _moe_kernel(T, cnt_ref, h3_ref, rt_ref, gcol_ref, w1_ref, w3_ref, w2_ref, y_ref,
                xs_ref, gs_ref, ys_ref, sl_ref, gl_ref):
    G, CH, SBK, WIN, RT = N_GROUPS, MOE_CHUNK, SORT_BLOCK, SORT_WIN, ROW_TILE_BF16
    b, g, hf = pl.program_id(0), pl.program_id(1), pl.program_id(2)
    nsb = T // SBK
    n_half = pl.num_programs(2)
    D = h3_ref.shape[1]

    pc = [[((cnt_ref[(b * nsb + j) * G + gg] + (RT - 1)) // RT) * RT for j in range(nsb)] for gg in range(G)]
    base = [0]
    for gg in range(G):
        base.append(base[-1] + functools.reduce(lambda u, w: u + w, pc[gg]))

    def seg_starts(j):
        out, src = [], 0
        for gg in range(G):
            dst = base[gg]
            for jp in range(j):
                dst = dst + pc[gg][jp]
            out.append((src, dst))
            src = src + pc[gg][j]
        return out

    @pl.when((g == 0) & (hf == 0))
    def _sort():
        total = pl.multiple_of(base[G], RT)
        xs_ref[pl.ds(total, CH), :] = jnp.zeros((CH, D), BF16)
        gs_ref[pl.ds(total, CH), :] = jnp.zeros((CH, LANES), F32)
        riota = lax.broadcasted_iota(I32, (WIN, SBK), 0)
        for j in range(nsb):
            lp = rt_ref[5:6, j * SBK:(j + 1) * SBK].astype(I32)
            perm = jnp.where(riota == lp, 1.0, 0.0).astype(BF16)
            sl_ref[...] = _dot(perm, h3_ref[j * SBK:(j + 1) * SBK, :]).astype(BF16)
            g_hi, g_lo = _split(gcol_ref[j * SBK:(j + 1) * SBK, :])
            gl_ref[...] = _dot(perm, g_hi) + _dot(perm, g_lo)
            for gg, (src, dst) in enumerate(seg_starts(j)):
                def copy_in(c, carry, src=src, dst=dst):
                    s = pl.multiple_of(src + c * RT, RT)
                    t = pl.multiple_of(dst + c * RT, RT)
                    xs_ref[pl.ds(t, RT), :] = sl_ref[pl.ds(s, RT), :]
                    gs_ref[pl.ds(t, RT), :] = gl_ref[pl.ds(s, RT), :]
                    return carry
                lax.fori_loop(0, pc[gg][j] // RT, copy_in, 0)

    rows_g = base[1] - base[0]
    base_g = base[0]
    for gg in range(1, G):
        rows_g = jnp.where(g == gg, base[gg + 1] - base[gg], rows_g)
        base_g = jnp.where(g == gg, base[gg], base_g)

    def chunk(c, carry):
        r0 = pl.multiple_of(base_g + c * CH, RT)
        xc = xs_ref[pl.ds(r0, CH), :]
        gc = gs_ref[pl.ds(r0, CH), :]
        acc = None
        for e in range(EXPERTS_PER_STEP):
            a = _dot(xc, w1_ref[e])
            u = _dot(xc, w3_ref[e])
            gate = gc[:, e:e + 1]
            for k in range(1, EXP_PER_GROUP // EXPERTS_PER_STEP):
                gate = jnp.where(hf == k, gc[:, k * EXPERTS_PER_STEP + e:k * EXPERTS_PER_STEP + e + 1], gate)
            hm = (a * _sigmoid(a)) * u * gate
            part = _dot(hm.astype(BF16), w2_ref[e])
            acc = part if acc is None else acc + part

        @pl.when(hf == 0)
        def _():
            ys_ref[pl.ds(r0, CH), :] = acc

        @pl.when(hf != 0)
        def _():
            ys_ref[pl.ds(r0, CH), :] += acc

        return carry

    lax.fori_loop(0, (rows_g + (CH - 1)) // CH, chunk, 0)

    @pl.when((g == G - 1) & (hf == n_half - 1))
    def _unsort():
        ciota = lax.broadcasted_iota(I32, (SBK, WIN), 1)
        for j in range(nsb):
            for gg, (src, dst) in enumerate(seg_starts(j)):
                def copy_out(c, carry, src=src, dst=dst):
                    s = pl.multiple_of(src + c * RT, RT)
                    t = pl.multiple_of(dst + c * RT, RT)
                    sl_ref[pl.ds(s, RT), :] = ys_ref[pl.ds(t, RT), :].astype(BF16)
                    return carry
                lax.fori_loop(0, pc[gg][j] // RT, copy_out, 0)
            lp = gcol_ref[j * SBK:(j + 1) * SBK, 5:6].astype(I32)
            perm_t = jnp.where(ciota == lp, 1.0, 0.0).astype(BF16)
            y_ref[j * SBK:(j + 1) * SBK, :] = _dot(perm_t, sl_ref[...]).astype(BF16)


def _moe(l, T, counts, h3, rt, gcol, w1, w3, w2):
    N, D = h3.shape
    G, E, EPS_ = N_GROUPS, EXP_PER_GROUP, EXPERTS_PER_STEP
    F = w1.shape[-1]
    rows = T + (T // SORT_BLOCK) * G * (ROW_TILE_BF16 - 1) + MOE_CHUNK
    rows = -(-rows // ROW_TILE_BF16) * ROW_TILE_BF16
    grid_spec = pltpu.PrefetchScalarGridSpec(
        num_scalar_prefetch=1,
        grid=(N // T, G, E // EPS_),
        in_specs=[
            pl.BlockSpec((T, D), lambda b, g, hf, cnt: (b, 0)),
            pl.BlockSpec((8, T), lambda b, g, hf, cnt: (0, b)),
            pl.BlockSpec((T, LANES), lambda b, g, hf, cnt: (b, 0)),
            pl.BlockSpec((None, None, EPS_, D, F), lambda b, g, hf, cnt: (l, g, hf, 0, 0)),
            pl.BlockSpec((None, None, EPS_, D, F), lambda b, g, hf, cnt: (l, g, hf, 0, 0)),
            pl.BlockSpec((None, None, EPS_, F, D), lambda b, g, hf, cnt: (l, g, hf, 0, 0)),
        ],
        out_specs=pl.BlockSpec((T, D), lambda b, g, hf, cnt: (b, 0)),
        scratch_shapes=[
            pltpu.VMEM((rows, D), BF16),
            pltpu.VMEM((rows, LANES), F32),
            pltpu.VMEM((rows, D), F32),
            pltpu.VMEM((SORT_WIN, D), BF16),
            pltpu.VMEM((SORT_WIN, LANES), F32),
        ],
    )
    return pl.pallas_call(
        functools.partial(_moe_kernel, T),
        grid_spec=grid_spec,
        out_shape=jax.ShapeDtypeStruct((N, D), BF16),
        compiler_params=_params(("arbitrary", "arbitrary", "arbitrary")),
        name="moe",
    )(counts, h3, rt, gcol, w1, w3, w2)


def _final_kernel(x_ref, y_ref, g_ref, o_ref):
    o_ref[...] = _rms(x_ref[...] + y_ref[...].astype(F32), g_ref[...])


def _final(x2, y, ln_final):
    N, D = x2.shape
    tm = TOKEN_TILE
    return pl.pallas_call(
        _final_kernel,
        grid=(N // tm,),
        in_specs=[pl.BlockSpec((tm, D), lambda i: (i, 0)), pl.BlockSpec((tm, D), lambda i: (i, 0)),
                  pl.BlockSpec((1, D), lambda i: (0, 0))],
        out_specs=pl.BlockSpec((tm, D), lambda i: (i, 0)),
        out_shape=jax.ShapeDtypeStruct((N, D), F32),
        compiler_params=_params(("arbitrary",)),
        name="final_norm",
    )(x2, y, ln_final.reshape(1, D))


def kernel(x, mem, ln_mix, w_in, conv_w, conv_b, i_bias, f_bias, ml_head_g, sb_head_g, w_out, ln_mem, ln_memkv,
           w_mq, w_mk, w_mv, w_mo, ln_ffn, w_rg, b_rg, w_re, b_re, w_e1, w_e3, w_e2, ln_final):
    B, S, D = x.shape
    N = B * S
    depth = w_in.shape[0]
    G, E = N_GROUPS, EXP_PER_GROUP
    assert S % TOKEN_TILE == 0 and S % ML_CHUNK == 0 and TOKEN_TILE % SORT_BLOCK == 0
    T = 2048 if S % 2048 == 0 else S

    c0, c1 = 4 * ML_WIDTH, 4 * ML_WIDTH + 2 * ML_HEADS
    w_ml = w_in[:, :, :c0].astype(BF16)
    w_sb = w_in[:, :, c1:].astype(BF16)
    w_gate = w_in[:, :, c0:c1]
    w_g = jnp.pad(w_gate, ((0, 0), (0, 0), (0, LANES - 8))).astype(BF16)
    w_gt = jnp.swapaxes(w_gate, 1, 2).astype(BF16)
    gate_b = jnp.concatenate([i_bias, f_bias], axis=-1)
    b_g = jnp.pad(gate_b, ((0, 0), (0, LANES - 8))).reshape(depth, 1, LANES)
    b_gt = gate_b.reshape(depth, 8, 1)
    w_r = jnp.pad(jnp.concatenate([w_rg, w_re], axis=-1), ((0, 0), (0, 0), (0, LANES - G - G * E)))
    wr_hi = w_r.astype(BF16)
    wr_lo = (w_r - wr_hi.astype(F32)).astype(BF16)
    b_r = jnp.pad(jnp.concatenate([b_rg, b_re], axis=-1), ((0, 0), (0, LANES - G - G * E))).reshape(depth, 1, LANES)
    w_out_b, w_mq_b, w_mo_b = w_out.astype(BF16), w_mq.astype(BF16), w_mo.astype(BF16)
    w1, w3, w2 = w_e1.astype(BF16), w_e3.astype(BF16), w_e2.astype(BF16)
    r3 = lambda a: a.reshape(depth, 1, a.shape[-1])

    kx, vx = _memkv(mem, ln_memkv, w_mk.astype(BF16), w_mv.astype(BF16))

    x2, y = x.reshape(N, D), None
    nsub = TOKEN_TILE // SORT_BLOCK
    for l in range(depth):
        ml, sb, gc, gt = _proj(l, x2, y, r3(ln_mix), w_ml, w_sb, w_g, w_gt, b_g, b_gt)
        hml = _mlstm(l, B, S, ml, gc, gt, conv_w, r3(conv_b), r3(ml_head_g))
        hsb = _stickbreak(l, B, S, sb, r3(sb_head_g))
        x2, h3, rt, gcol, cnt = _post(l, B, S, x2, y, hml, hsb, w_out_b, r3(ln_mem), w_mq_b, kx, vx, w_mo_b,
                                      r3(ln_ffn), wr_hi, wr_lo, b_r)
        counts = jnp.swapaxes(cnt[:, :G, :nsub], 1, 2).reshape(-1)
        y = _moe(l, T, counts, h3, rt, gcol, w1, w3, w2)
    return _final(x2, y, ln_final).reshape(B, S, D)
```

```python
import functools

import jax
import jax.numpy as jnp
from jax import lax
from jax.experimental import pallas as pl
from jax.experimental.pallas import tpu as pltpu

F32 = jnp.float32
BF16 = jnp.bfloat16
I32 = jnp.int32

ML_HEADS = 4
ML_DHEAD = 128
SB_HEADS = 4
SB_DHEAD = 128
ML_WIDTH = ML_HEADS * ML_DHEAD
SB_WIDTH = SB_HEADS * SB_DHEAD
CONV_W = 4
ML_CHUNK = 128
SB_SPAN = 256
XA_HEADS = 4
N_GROUPS = 4
EXP_PER_GROUP = 4
EPS = 1e-6

LANES = 128
ROW_TILE_BF16 = 16
SORT_BLOCK = 256
SORT_WIN = 384
MOE_CHUNK = 128
EXPERTS_PER_STEP = 2
TOKEN_TILE = 512
VMEM_LIMIT = 56 * 1024 * 1024


def _dot(a, b):
    return jnp.dot(a, b, preferred_element_type=F32)


def _dot_nt(a, b):
    return lax.dot_general(a, b, (((1,), (1,)), ((), ())), preferred_element_type=F32)


def _split(x):
    hi = x.astype(BF16)
    lo = (x - hi.astype(F32)).astype(BF16)
    return hi, lo


def _rms(x, g):
    return x * lax.rsqrt(jnp.mean(x * x, axis=-1, keepdims=True) + EPS) * g


def _sigmoid(x):
    return 1.0 / (1.0 + jnp.exp(-x))


def _softplus(x):
    return jnp.maximum(x, 0.0) + jnp.log1p(jnp.exp(-jnp.abs(x)))


def _params(sem):
    return pltpu.CompilerParams(dimension_semantics=sem, vmem_limit_bytes=VMEM_LIMIT)


def _memkv_kernel(mem_ref, g_ref, wk_ref, wv_ref, k_ref, v_ref):
    h = _rms(mem_ref[0], g_ref[0]).astype(BF16)
    k_ref[0, 0] = _dot(h, wk_ref[0]).astype(BF16)
    v_ref[0, 0] = _dot(h, wv_ref[0]).astype(BF16)


def _memkv(mem, ln_memkv, w_mk, w_mv):
    B, M, D = mem.shape
    depth = w_mk.shape[0]
    return pl.pallas_call(
        _memkv_kernel,
        grid=(depth, B),
        in_specs=[
            pl.BlockSpec((1, M, D), lambda l, b: (b, 0, 0)),
            pl.BlockSpec((1, 1, D), lambda l, b: (l, 0, 0)),
            pl.BlockSpec((1, D, D), lambda l, b: (l, 0, 0)),
            pl.BlockSpec((1, D, D), lambda l, b: (l, 0, 0)),
        ],
        out_specs=[
            pl.BlockSpec((1, 1, M, D), lambda l, b: (l, b, 0, 0)),
            pl.BlockSpec((1, 1, M, D), lambda l, b: (l, b, 0, 0)),
        ],
        out_shape=[jax.ShapeDtypeStruct((depth, B, M, D), BF16)] * 2,
        compiler_params=_params(("arbitrary", "arbitrary")),
        name="memkv",
    )(mem, ln_memkv.reshape(depth, 1, D), w_mk, w_mv)


def _proj_kernel(has_y, tiles_per_seq, *refs):
    if has_y:
        x_ref, y_ref = refs[:2]
        refs = refs[2:]
        x = x_ref[...] + y_ref[...].astype(F32)
    else:
        x_ref = refs[0]
        refs = refs[1:]
        x = x_ref[...]
    (g_ref, wml_ref, wsb_ref, wg_ref, wgt_ref, bg_ref, bgt_ref, cw_ref, cb_ref,
     ml_ref, sb_ref, gc_ref, gt_ref, prev_ref) = refs
    tm, W, L = x.shape[0], ML_WIDTH, ML_CHUNK

    @pl.when(pl.program_id(0) % tiles_per_seq == 0)
    def _():
        prev_ref[...] = jnp.zeros_like(prev_ref)

    h = _rms(x, g_ref[0]).astype(BF16)
    qk = _dot(h, wml_ref[0, :, :2 * W])
    ext = jnp.concatenate([prev_ref[...], qk], axis=0)
    prev_ref[...] = qk[tm - 8:, :]
    cw = cw_ref[0]
    acc = cb_ref[0]
    for tap in range(CONV_W):
        lo = 8 - (CONV_W - 1) + tap
        acc = acc + ext[lo:lo + tm, :] * cw[tap:tap + 1, :]
    qa = acc * _sigmoid(acc)
    ml_ref[:, :W] = qa[:, :W].astype(BF16)
    ml_ref[:, W:2 * W] = (qa[:, W:] * (ML_DHEAD ** -0.5)).astype(BF16)
    ml_ref[:, 2 * W:] = _dot(h, wml_ref[0, :, 2 * W:]).astype(BF16)
    sbp = _dot(h, wsb_ref[0])
    sb_ref[:, :SB_WIDTH] = (sbp[:, :SB_WIDTH] * (SB_DHEAD ** -0.5)).astype(BF16)
    sb_ref[:, SB_WIDTH:] = sbp[:, SB_WIDTH:].astype(BF16)
    gc_ref[...] = _dot(h, wg_ref[0]) + bg_ref[0]
    gt = _dot_nt(wgt_ref[0], h) + bgt_ref[0]
    for c in range(tm // L):
        gt_ref[c] = gt[:, c * L:(c + 1) * L]


def _proj(l, S, x2, y, ln_mix, w_ml, w_sb, w_g, w_gt, b_g, b_gt, conv_w, conv_b):
    N, D = x2.shape
    tm, W, L = TOKEN_TILE, ML_WIDTH, ML_CHUNK
    row = lambda i: (i, 0)
    lay = lambda i: (l, 0, 0)
    ins = [x2] + ([y] if y is not None else [])
    in_specs = [pl.BlockSpec((tm, D), row)] * len(ins) + [
        pl.BlockSpec((1, 1, D), lay),
        pl.BlockSpec((1, D, 4 * W), lay),
        pl.BlockSpec((1, D, 3 * SB_WIDTH), lay),
        pl.BlockSpec((1, D, LANES), lay),
        pl.BlockSpec((1, 8, D), lay),
        pl.BlockSpec((1, 1, LANES), lay),
        pl.BlockSpec((1, 8, 1), lay),
        pl.BlockSpec((1, CONV_W, 2 * W), lay),
        pl.BlockSpec((1, 1, 2 * W), lay),
    ]
    return pl.pallas_call(
        functools.partial(_proj_kernel, y is not None, S // tm),
        grid=(N // tm,),
        in_specs=in_specs,
        out_specs=[
            pl.BlockSpec((tm, 4 * W), row),
            pl.BlockSpec((tm, 3 * SB_WIDTH), row),
            pl.BlockSpec((tm, LANES), row),
            pl.BlockSpec((tm // L, 8, L), lambda i: (i, 0, 0)),
        ],
        out_shape=[
            jax.ShapeDtypeStruct((N, 4 * W), BF16),
            jax.ShapeDtypeStruct((N, 3 * SB_WIDTH), BF16),
            jax.ShapeDtypeStruct((N, LANES), F32),
            jax.ShapeDtypeStruct((N // L, 8, L), F32),
        ],
        scratch_shapes=[pltpu.VMEM((8, 2 * W), F32)],
        compiler_params=_params(("arbitrary",)),
        name="proj",
    )(*ins, ln_mix, w_ml, w_sb, w_g, w_gt, b_g, b_gt, conv_w, conv_b)


def _mlstm_kernel(ml_ref, gc_ref, gt_ref, hg_ref, out_ref, ct_ref, n_ref, m_ref):
    L, d, W, H = ML_CHUNK, ML_DHEAD, ML_WIDTH, ML_HEADS
    nb = ml_ref.shape[0]

    @pl.when(pl.program_id(1) == 0)
    def _():
        ct_ref[...] = jnp.zeros_like(ct_ref)
        n_ref[...] = jnp.zeros_like(n_ref)
        m_ref[...] = jnp.zeros_like(m_ref)

    row = lax.broadcasted_iota(I32, (L, L), 0)
    col = lax.broadcasted_iota(I32, (L, L), 1)
    causal = col <= row
    tri = jnp.where(causal, 1.0, 0.0).astype(BF16)
    tri_t = jnp.where(row <= col, 1.0, 0.0).astype(BF16)
    hg = hg_ref[0]

    for bb in range(nb):
        gt = gt_ref[bb, 0]
        gc = gc_ref[bb]
        lsc_hi, lsc_lo = _split(-_softplus(-gc))
        bc = _dot(tri, lsc_hi) + _dot(tri, lsc_lo)
        lst_hi, lst_lo = _split(-_softplus(-gt))
        bt = _dot(lst_hi, tri_t) + _dot(lst_lo, tri_t)
        for h in range(H):
            st = bb * H + h
            qb = ml_ref[bb, :, h * d:(h + 1) * d]
            kb = ml_ref[bb, :, W + h * d:W + (h + 1) * d]
            v = ml_ref[bb, :, 2 * W + h * d:2 * W + (h + 1) * d]
            og = ml_ref[bb, :, 3 * W + h * d:3 * W + (h + 1) * d].astype(F32)
            i_r = gt[h:h + 1, :]
            b_r = bt[H + h:H + h + 1, :]
            i_c = gc[:, h:h + 1]
            b_c = bc[:, H + h:H + h + 1]
            m_prev = m_ref[st][:1, :1]
            logd = jnp.where(causal, b_c - b_r + i_r, -jnp.inf)
            inter = b_c + m_prev
            m_t = jnp.maximum(inter, jnp.max(logd, axis=-1, keepdims=True))
            s = _dot_nt(qb, kb) * jnp.exp(logd - m_t)
            w_inter = jnp.exp(inter - m_t)
            ct = ct_ref[st]
            n_row = n_ref[st][:1, :]
            num = _dot(s.astype(BF16), v) + w_inter * _dot(qb, ct.astype(BF16))
            den = (jnp.sum(s, axis=-1, keepdims=True)
                   + w_inter * jnp.sum(qb.astype(F32) * n_row, axis=-1, keepdims=True))
            hh = num / jnp.maximum(jnp.abs(den), jnp.exp(-m_t))

            b_last = b_r[:, L - 1:L]
            m_new = jnp.maximum(b_last + m_prev, jnp.max(b_last - b_r + i_r, axis=-1, keepdims=True))
            w_old = jnp.exp(b_last + m_prev - m_new)
            kw = kb.astype(F32) * jnp.exp(b_last - b_c + i_c - m_new)
            ct_ref[st] = w_old * ct + _dot(kw.T.astype(BF16), v)
            n_ref[st] = jnp.broadcast_to(w_old * n_row + jnp.sum(kw, axis=0, keepdims=True), (8, d))
            m_ref[st] = jnp.broadcast_to(m_new, (8, LANES))

            hn = hh * lax.rsqrt(jnp.mean(hh * hh, axis=-1, keepdims=True) + EPS) * hg[:, h * d:(h + 1) * d]
            out_ref[bb, :, h * d:(h + 1) * d] = (_sigmoid(og) * hn).astype(BF16)


def _mlstm(l, B, S, ml, gc, gt, ml_head_g):
    L, W, H, d = ML_CHUNK, ML_WIDTH, ML_HEADS, ML_DHEAD
    nc = S // L
    nb = 4 if B % 4 == 0 else (2 if B % 2 == 0 else 1)
    out = pl.pallas_call(
        _mlstm_kernel,
        grid=(B // nb, nc),
        in_specs=[
            pl.BlockSpec((nb, L, 4 * W), lambda b, c: (b, c, 0)),
            pl.BlockSpec((nb, L, LANES), lambda b, c: (b, c, 0)),
            pl.BlockSpec((nb, 1, 8, L), lambda b, c: (b, c, 0, 0)),
            pl.BlockSpec((1, 1, W), lambda b, c: (l, 0, 0)),
        ],
        out_specs=pl.BlockSpec((nb, L, W), lambda b, c: (b, c, 0)),
        out_shape=jax.ShapeDtypeStruct((B, S, W), BF16),
        scratch_shapes=[
            pltpu.VMEM((nb * H, d, d), F32),
            pltpu.VMEM((nb * H, 8, d), F32),
            pltpu.VMEM((nb * H, 8, LANES), F32),
        ],
        compiler_params=_params(("arbitrary", "arbitrary")),
        name="mlstm",
    )(ml.reshape(B, S, 4 * W), gc.reshape(B, S, LANES), gt.reshape(B, nc, 8, L), ml_head_g)
    return out.reshape(B * S, W)


def _sb_kernel(q_ref, k_ref, v_ref, g_ref, o_ref, acc_ref):
    P, H, d = SB_SPAN, SB_HEADS, SB_DHEAD
    i = pl.program_id(1)
    row = lax.broadcasted_iota(I32, (P, P), 0)
    col = lax.broadcasted_iota(I32, (P, P), 1)
    strict = col < row
    after = jnp.where(row > col, 1.0, 0.0).astype(BF16)

    def span(j, rsums, diagonal):
        start = pl.multiple_of(j * P, P)
        out = []
        for h in range(H):
            hs = slice(h * d, (h + 1) * d)
            z = _dot_nt(q_ref[:, hs], k_ref[pl.ds(start, P), hs])
            sp = jnp.maximum(z, 0.0) + jnp.log(1.0 + jnp.exp(-jnp.abs(z)))
            if diagonal:
                sp = jnp.where(strict, sp, 0.0)
            hi, lo = _split(sp)
            tail = _dot(hi, after) + _dot(lo, after)
            if diagonal:
                a = jnp.where(strict, jnp.exp(z - sp - tail), 0.0)
                acc_ref[:, hs] = _dot(a.astype(BF16), v_ref[pl.ds(start, P), hs])
                out.append(tail[:, 0:1] + sp[:, 0:1])
            else:
                a = jnp.exp(z - sp - tail - rsums[h])
                acc_ref[:, hs] += _dot(a.astype(BF16), v_ref[pl.ds(start, P), hs])
                out.append(rsums[h] + tail[:, 0:1] + sp[:, 0:1])
        return tuple(out)

    rsums = span(i, None, True)
    lax.fori_loop(0, i, lambda jj, r: span(i - 1 - jj, r, False), rsums)
    g = g_ref[0]
    for h in range(H):
        hs = slice(h * d, (h + 1) * d)
        acc = acc_ref[:, hs]
        o = acc * lax.rsqrt(jnp.mean(acc * acc, axis=-1, keepdims=True) + EPS) * g[:, hs]
        o_ref[:, hs] = o.astype(BF16)


def _stickbreak(l, B, S, sb, sb_head_g):
    P, W = SB_SPAN, SB_WIDTH
    nq = S // P
    return pl.pallas_call(
        _sb_kernel,
        grid=(B, nq),
        in_specs=[
            pl.BlockSpec((P, W), lambda b, i: (b * nq + i, 0)),
            pl.BlockSpec((S, W), lambda b, i: (b, 1)),
            pl.BlockSpec((S, W), lambda b, i: (b, 2)),
            pl.BlockSpec((1, 1, W), lambda b, i: (l, 0, 0)),
        ],
        out_specs=pl.BlockSpec((P, W), lambda b, i: (b * nq + i, 0)),
        out_shape=jax.ShapeDtypeStruct((B * S, W), BF16),
        scratch_shapes=[pltpu.VMEM((P, W), F32)],
        compiler_params=_params(("arbitrary", "arbitrary")),
        name="stickbreak",
    )(sb, sb, sb, sb_head_g)


def _route(lgt, tm):
    G, E = N_GROUPS, EXP_PER_GROUP
    lg = [lgt[i:i + 1, :] for i in range(G + G * E)]
    mg = functools.reduce(jnp.maximum, lg[:G])
    eg = [jnp.exp(v - mg) for v in lg[:G]]
    zg = functools.reduce(jnp.add, eg)
    pg = [v / zg for v in eg]
    p_sel = functools.reduce(jnp.maximum, pg)
    gid = jnp.where(pg[0] == p_sel, 0, jnp.where(pg[1] == p_sel, 1, jnp.where(pg[2] == p_sel, 2, 3)))
    es = [jnp.where(gid == 0, lg[G + e], jnp.where(gid == 1, lg[G + E + e],
          jnp.where(gid == 2, lg[G + 2 * E + e], lg[G + 3 * E + e]))) for e in range(E)]
    me = functools.reduce(jnp.maximum, es)
    ee = [jnp.exp(v - me) for v in es]
    ze = functools.reduce(jnp.add, ee)
    pe = [v / ze for v in ee]

    def first_max(vals):
        top = functools.reduce(jnp.maximum, vals)
        idx = jnp.where(vals[0] == top, 0, jnp.where(vals[1] == top, 1, jnp.where(vals[2] == top, 2, 3)))
        return top, idx

    v1, i1 = first_max(pe)
    v2, i2 = first_max([jnp.where(i1 == e, -1.0, pe[e]) for e in range(E)])
    tsum = v1 + v2
    tw1, tw2 = v1 / tsum, v2 / tsum
    gates = [p_sel * (jnp.where(i1 == e, tw1, 0.0) + jnp.where(i2 == e, tw2, 0.0)) for e in range(E)]

    sub8 = lax.broadcasted_iota(I32, (8, tm), 0)
    onehot = jnp.where(sub8 == gid, 1.0, 0.0)
    r = lax.broadcasted_iota(I32, (SORT_BLOCK, SORT_BLOCK), 0)
    c = lax.broadcasted_iota(I32, (SORT_BLOCK, SORT_BLOCK), 1)
    before = jnp.where(r < c, 1.0, 0.0).astype(BF16)
    lane = lax.broadcasted_iota(I32, (8, LANES), 1)
    counts = jnp.zeros((8, LANES), F32)
    lps = []
    for jj in range(tm // SORT_BLOCK):
        oj = onehot[:, jj * SORT_BLOCK:(jj + 1) * SORT_BLOCK]
        pre = _dot(oj.astype(BF16), before)
        cnt = jnp.sum(oj, axis=-1, keepdims=True)
        pcnt = jnp.floor((cnt + (ROW_TILE_BF16 - 1.0)) * (1.0 / ROW_TILE_BF16)) * ROW_TILE_BF16
        start = jnp.zeros((1, 1), F32)
        lp = jnp.zeros((1, SORT_BLOCK), F32)
        for g in range(G):
            lp = lp + oj[g:g + 1, :] * (pre[g:g + 1, :] + start)
            start = start + pcnt[g:g + 1, :]
        lps.append(lp)
        counts = counts + jnp.where(lane == jj, cnt, 0.0)
    lp = jnp.concatenate(lps, axis=1)
    rows = gates + [gid.astype(F32), lp]
    pack = jnp.zeros((8, tm), F32)
    for kk, v in enumerate(rows):
        pack = pack + jnp.where(sub8 == kk, v, 0.0)
    return pack, counts.astype(I32)


def _post_kernel(has_y, *refs):
    if has_y:
        x_ref, y_ref = refs[:2]
        refs = refs[2:]
        x = x_ref[...] + y_ref[...].astype(F32)
    else:
        x_ref = refs[0]
        refs = refs[1:]
        x = x_ref[...]
    (hml_ref, hsb_ref, wo_ref, gmem_ref, wq_ref, k_ref, v_ref, wmo_ref, gffn_ref, wrh_ref, wrl_ref, br_ref,
     x2_ref, h3_ref, rt_ref, gcol_ref, cnt_ref) = refs
    tm = x.shape[0]
    x1 = x + _dot(hml_ref[...], wo_ref[0, :ML_WIDTH, :]) + _dot(hsb_ref[...], wo_ref[0, ML_WIDTH:, :])
    q = _dot(_rms(x1, gmem_ref[0]).astype(BF16), wq_ref[0]).astype(BF16)
    kk = k_ref[0, 0]
    vv = v_ref[0, 0]
    dh = q.shape[1] // XA_HEADS
    outs = []
    for hd in range(XA_HEADS):
        sl = slice(hd * dh, (hd + 1) * dh)
        s = _dot_nt(q[:, sl], kk[:, sl]) * (dh ** -0.5)
        p = jnp.exp(s - jnp.max(s, axis=-1, keepdims=True))
        p = p / jnp.sum(p, axis=-1, keepdims=True)
        outs.append(_dot(p.astype(BF16), vv[:, sl]).astype(BF16))
    x2 = x1 + _dot(jnp.concatenate(outs, axis=-1), wmo_ref[0])
    x2_ref[...] = x2
    h3_hi, h3_lo = _split(_rms(x2, gffn_ref[0]))
    h3_ref[...] = h3_hi
    wrh = wrh_ref[0]
    lg = _dot(h3_hi, wrh) + _dot(h3_lo, wrh) + _dot(h3_hi, wrl_ref[0]) + br_ref[0]
    pack, counts = _route(lg.T, tm)
    rt_ref[...] = pack
    gcol_ref[...] = jnp.concatenate([pack, jnp.zeros((LANES - 8, tm), F32)], axis=0).T
    cnt_ref[0] = counts


def _post(l, B, S, x2, y, hml, hsb, w_out, ln_mem, w_mq, kx, vx, w_mo, ln_ffn, wr_hi, wr_lo, b_r):
    N, D = x2.shape
    M = kx.shape[2]
    tm = TOKEN_TILE
    tpb = S // tm
    row = lambda i: (i, 0)
    lay = lambda i: (l, 0, 0)
    ins = [x2] + ([y] if y is not None else [])
    in_specs = [pl.BlockSpec((tm, D), row)] * len(ins) + [
        pl.BlockSpec((tm, ML_WIDTH), row),
        pl.BlockSpec((tm, SB_WIDTH), row),
        pl.BlockSpec((1, D, D), lay),
        pl.BlockSpec((1, 1, D), lay),
        pl.BlockSpec((1, D, D), lay),
        pl.BlockSpec((1, 1, M, D), lambda i: (l, i // tpb, 0, 0)),
        pl.BlockSpec((1, 1, M, D), lambda i: (l, i // tpb, 0, 0)),
        pl.BlockSpec((1, D, D), lay),
        pl.BlockSpec((1, 1, D), lay),
        pl.BlockSpec((1, D, LANES), lay),
        pl.BlockSpec((1, D, LANES), lay),
        pl.BlockSpec((1, 1, LANES), lay),
    ]
    return pl.pallas_call(
        functools.partial(_post_kernel, y is not None),
        grid=(N // tm,),
        in_specs=in_specs,
        out_specs=[
            pl.BlockSpec((tm, D), row),
            pl.BlockSpec((tm, D), row),
            pl.BlockSpec((8, tm), lambda i: (0, i)),
            pl.BlockSpec((tm, LANES), row),
            pl.BlockSpec((1, 8, LANES), lambda i: (i, 0, 0)),
        ],
        out_shape=[
            jax.ShapeDtypeStruct((N, D), F32),
            jax.ShapeDtypeStruct((N, D), BF16),
            jax.ShapeDtypeStruct((8, N), F32),
            jax.ShapeDtypeStruct((N, LANES), F32),
            jax.ShapeDtypeStruct((N // tm, 8, LANES), I32),
        ],
        compiler_params=_params(("arbitrary",)),
        name="post",
    )(*ins, hml, hsb, w_out, ln_mem, w_mq, kx, vx, w_mo, ln_ffn, wr_hi, wr_lo, b_r)


def _moe_kernel(T, cnt_ref, h3_ref, rt_ref, gcol_ref, w1_ref, w3_ref, w2_ref, y_ref,
                xs_ref, gs_ref, ys_ref, sl_ref, gl_ref):
    G, CH, SBK, WIN, RT = N_GROUPS, MOE_CHUNK, SORT_BLOCK, SORT_WIN, ROW_TILE_BF16
    b, g, hf = pl.program_id(0), pl.program_id(1), pl.program_id(2)
    nsb = T // SBK
    n_half = pl.num_programs(2)
    D = h3_ref.shape[1]

    pc = [[((cnt_ref[(b * nsb + j) * G + gg] + (RT - 1)) // RT) * RT for j in range(nsb)] for gg in range(G)]
    base = [0]
    for gg in range(G):
        base.append(base[-1] + functools.reduce(lambda u, w: u + w, pc[gg]))

    def seg_starts(j):
        out, src = [], 0
        for gg in range(G):
            dst = base[gg]
            for jp in range(j):
                dst = dst + pc[gg][jp]
            out.append((src, dst))
            src = src + pc[gg][j]
        return out

    @pl.when((g == 0) & (hf == 0))
    def _sort():
        total = pl.multiple_of(base[G], RT)
        xs_ref[pl.ds(total, CH), :] = jnp.zeros((CH, D), BF16)
        gs_ref[pl.ds(total, CH), :] = jnp.zeros((CH, LANES), F32)
        riota = lax.broadcasted_iota(I32, (WIN, SBK), 0)
        for j in range(nsb):
            lp = rt_ref[5:6, j * SBK:(j + 1) * SBK].astype(I32)
            perm = jnp.where(riota == lp, 1.0, 0.0).astype(BF16)
            sl_ref[...] = _dot(perm, h3_ref[j * SBK:(j + 1) * SBK, :]).astype(BF16)
            g_hi, g_lo = _split(gcol_ref[j * SBK:(j + 1) * SBK, :])
            gl_ref[...] = _dot(perm, g_hi) + _dot(perm, g_lo)
            for gg, (src, dst) in enumerate(seg_starts(j)):
                def copy_in(c, carry, src=src, dst=dst):
                    s = pl.multiple_of(src + c * RT, RT)
                    t = pl.multiple_of(dst + c * RT, RT)
                    xs_ref[pl.ds(t, RT), :] = sl_ref[pl.ds(s, RT), :]
                    gs_ref[pl.ds(t, RT), :] = gl_ref[pl.ds(s, RT), :]
                    return carry
                lax.fori_loop(0, pc[gg][j] // RT, copy_in, 0)

    rows_g = base[1] - base[0]
    base_g = base[0]
    for gg in range(1, G):
        rows_g = jnp.where(g == gg, base[gg + 1] - base[gg], rows_g)
        base_g = jnp.where(g == gg, base[gg], base_g)

    def chunk(c, carry):
        r0 = pl.multiple_of(base_g + c * CH, RT)
        xc = xs_ref[pl.ds(r0, CH), :]
        gc = gs_ref[pl.ds(r0, CH), :]
        acc = None
        for e in range(EXPERTS_PER_STEP):
            a = _dot(xc, w1_ref[e])
            u = _dot(xc, w3_ref[e])
            gate = gc[:, e:e + 1]
            for k in range(1, EXP_PER_GROUP // EXPERTS_PER_STEP):
                gate = jnp.where(hf == k, gc[:, k * EXPERTS_PER_STEP + e:k * EXPERTS_PER_STEP + e + 1], gate)
            hm = (a * _sigmoid(a)) * u * gate
            part = _dot(hm.astype(BF16), w2_ref[e])
            acc = part if acc is None else acc + part

        @pl.when(hf == 0)
        def _():
            ys_ref[pl.ds(r0, CH), :] = acc

        @pl.when(hf != 0)
        def _():
            ys_ref[pl.ds(r0, CH), :] += acc

        return carry

    lax.fori_loop(0, (rows_g + (CH - 1)) // CH, chunk, 0)

    @pl.when((g == G - 1) & (hf == n_half - 1))
    def _unsort():
        ciota = lax.broadcasted_iota(I32, (SBK, WIN), 1)
        for j in range(nsb):
            for gg, (src, dst) in enumerate(seg_starts(j)):
                def copy_out(c, carry, src=src, dst=dst):
                    s = pl.multiple_of(src + c * RT, RT)
                    t = pl.multiple_of(dst + c * RT, RT)
                    sl_ref[pl.ds(s, RT), :] = ys_ref[pl.ds(t, RT), :].astype(BF16)
                    return carry
                lax.fori_loop(0, pc[gg][j] // RT, copy_out, 0)
            lp = gcol_ref[j * SBK:(j + 1) * SBK, 5:6].astype(I32)
            perm_t = jnp.where(ciota == lp, 1.0, 0.0).astype(BF16)
            y_ref[j * SBK:(j + 1) * SBK, :] = _dot(perm_t, sl_ref[...]).astype(BF16)


def _moe(l, T, counts, h3, rt, gcol, w1, w3, w2):
    N, D = h3.shape
    G, E, EPS_ = N_GROUPS, EXP_PER_GROUP, EXPERTS_PER_STEP
    F = w1.shape[-1]
    rows = T + (T // SORT_BLOCK) * G * (ROW_TILE_BF16 - 1) + MOE_CHUNK
    rows = -(-rows // ROW_TILE_BF16) * ROW_TILE_BF16
    grid_spec = pltpu.PrefetchScalarGridSpec(
        num_scalar_prefetch=1,
        grid=(N // T, G, E // EPS_),
        in_specs=[
            pl.BlockSpec((T, D), lambda b, g, hf, cnt: (b, 0)),
            pl.BlockSpec((8, T), lambda b, g, hf, cnt: (0, b)),
            pl.BlockSpec((T, LANES), lambda b, g, hf, cnt: (b, 0)),
            pl.BlockSpec((None, None, EPS_, D, F), lambda b, g, hf, cnt: (l, g, hf, 0, 0)),
            pl.BlockSpec((None, None, EPS_, D, F), lambda b, g, hf, cnt: (l, g, hf, 0, 0)),
            pl.BlockSpec((None, None, EPS_, F, D), lambda b, g, hf, cnt: (l, g, hf, 0, 0)),
        ],
        out_specs=pl.BlockSpec((T, D), lambda b, g, hf, cnt: (b, 0)),
        scratch_shapes=[
            pltpu.VMEM((rows, D), BF16),
            pltpu.VMEM((rows, LANES), F32),
            pltpu.VMEM((rows, D), F32),
            pltpu.VMEM((SORT_WIN, D), BF16),
            pltpu.VMEM((SORT_WIN, LANES), F32),
        ],
    )
    return pl.pallas_call(
        functools.partial(_moe_kernel, T),
        grid_spec=grid_spec,
        out_shape=jax.ShapeDtypeStruct((N, D), BF16),
        compiler_params=_params(("arbitrary", "arbitrary", "arbitrary")),
        name="moe",
    )(counts, h3, rt, gcol, w1, w3, w2)


def _final_kernel(x_ref, y_ref, g_ref, o_ref):
    o_ref[...] = _rms(x_ref[...] + y_ref[...].astype(F32), g_ref[...])


def _final(x2, y, ln_final):
    N, D = x2.shape
    tm = TOKEN_TILE
    return pl.pallas_call(
        _final_kernel,
        grid=(N // tm,),
        in_specs=[pl.BlockSpec((tm, D), lambda i: (i, 0)), pl.BlockSpec((tm, D), lambda i: (i, 0)),
                  pl.BlockSpec((1, D), lambda i: (0, 0))],
        out_specs=pl.BlockSpec((tm, D), lambda i: (i, 0)),
        out_shape=jax.ShapeDtypeStruct((N, D), F32),
        compiler_params=_params(("arbitrary",)),
        name="final_norm",
    )(x2, y, ln_final.reshape(1, D))


def kernel(x, mem, ln_mix, w_in, conv_w, conv_b, i_bias, f_bias, ml_head_g, sb_head_g, w_out, ln_mem, ln_memkv,
           w_mq, w_mk, w_mv, w_mo, ln_ffn, w_rg, b_rg, w_re, b_re, w_e1, w_e3, w_e2, ln_final):
    B, S, D = x.shape
    N = B * S
    depth = w_in.shape[0]
    G, E = N_GROUPS, EXP_PER_GROUP
    assert S % TOKEN_TILE == 0 and TOKEN_TILE % ML_CHUNK == 0 and TOKEN_TILE % SORT_BLOCK == 0 and S % SB_SPAN == 0
    T = 2048 if S % 2048 == 0 else S

    c0, c1 = 4 * ML_WIDTH, 4 * ML_WIDTH + 2 * ML_HEADS
    w_ml = w_in[:, :, :c0].astype(BF16)
    w_sb = w_in[:, :, c1:].astype(BF16)
    w_gate = w_in[:, :, c0:c1]
    w_g = jnp.pad(w_gate, ((0, 0), (0, 0), (0, LANES - 8))).astype(BF16)
    w_gt = jnp.swapaxes(w_gate, 1, 2).astype(BF16)
    gate_b = jnp.concatenate([i_bias, f_bias], axis=-1)
    b_g = jnp.pad(gate_b, ((0, 0), (0, LANES - 8))).reshape(depth, 1, LANES)
    b_gt = gate_b.reshape(depth, 8, 1)
    w_r = jnp.pad(jnp.concatenate([w_rg, w_re], axis=-1), ((0, 0), (0, 0), (0, LANES - G - G * E)))
    wr_hi = w_r.astype(BF16)
    wr_lo = (w_r - wr_hi.astype(F32)).astype(BF16)
    b_r = jnp.pad(jnp.concatenate([b_rg, b_re], axis=-1), ((0, 0), (0, LANES - G - G * E))).reshape(depth, 1, LANES)
    w_out_b, w_mq_b, w_mo_b = w_out.astype(BF16), w_mq.astype(BF16), w_mo.astype(BF16)
    w1, w3, w2 = w_e1.astype(BF16), w_e3.astype(BF16), w_e2.astype(BF16)
    r3 = lambda a: a.reshape(depth, 1, a.shape[-1])

    kx, vx = _memkv(mem, ln_memkv, w_mk.astype(BF16), w_mv.astype(BF16))

    x2, y = x.reshape(N, D), None
    nsub = TOKEN_TILE // SORT_BLOCK
    for l in range(depth):
        ml, sb, gc, gt = _proj(l, S, x2, y, r3(ln_mix), w_ml, w_sb, w_g, w_gt, b_g, b_gt, conv_w, r3(conv_b))
        hml = _mlstm(l, B, S, ml, gc, gt, r3(ml_head_g))
        hsb = _stickbreak(l, B, S, sb, r3(sb_head_g))
        x2, h3, rt, gcol, cnt = _post(l, B, S, x2, y, hml, hsb, w_out_b, r3(ln_mem), w_mq_b, kx, vx, w_mo_b,
                                      r3(ln_ffn), wr_hi, wr_lo, b_r)
        counts = jnp.swapaxes(cnt[:, :G, :nsub], 1, 2).reshape(-1)
        y = _moe(l, T, counts, h3, rt, gcol, w1, w3, w2)
    return _final(x2, y, ln_final).reshape(B, S, D)
```

```python
import functools

import jax
import jax.numpy as jnp
from jax import lax
from jax.experimental import pallas as pl
from jax.experimental.pallas import tpu as pltpu

F32 = jnp.float32
BF16 = jnp.bfloat16
I32 = jnp.int32

ML_HEADS = 4
ML_DHEAD = 128
SB_HEADS = 4
SB_DHEAD = 128
ML_WIDTH = ML_HEADS * ML_DHEAD
SB_WIDTH = SB_HEADS * SB_DHEAD
CONV_W = 4
ML_CHUNK = 128
SB_SPAN = 256
XA_HEADS = 4
N_GROUPS = 4
EXP_PER_GROUP = 4
EPS = 1e-6
LOG2E = 1.4426950408889634

LANES = 128
ROW_TILE_BF16 = 16
SORT_BLOCK = 256
SORT_WIN = 384
MOE_CHUNK = 128
EXPERTS_PER_STEP = 2
TOKEN_TILE = 512
VMEM_LIMIT = 56 * 1024 * 1024


def _dot(a, b):
    return jnp.dot(a, b, preferred_element_type=F32)


def _dot_nt(a, b):
    return lax.dot_general(a, b, (((1,), (1,)), ((), ())), preferred_element_type=F32)


def _split(x):
    hi = x.astype(BF16)
    lo = (x - hi.astype(F32)).astype(BF16)
    return hi, lo


def _rms(x, g):
    return x * lax.rsqrt(jnp.mean(x * x, axis=-1, keepdims=True) + EPS) * g


def _sigmoid(x):
    return 1.0 / (1.0 + jnp.exp(-x))


def _softplus(x):
    return jnp.maximum(x, 0.0) + jnp.log1p(jnp.exp(-jnp.abs(x)))


def _params(sem):
    return pltpu.CompilerParams(dimension_semantics=sem, vmem_limit_bytes=VMEM_LIMIT)


def _memkv_kernel(mem_ref, g_ref, wk_ref, wv_ref, k_ref, v_ref):
    h = _rms(mem_ref[0], g_ref[0]).astype(BF16)
    k_ref[0, 0] = _dot(h, wk_ref[0]).astype(BF16)
    v_ref[0, 0] = _dot(h, wv_ref[0]).astype(BF16)


def _memkv(mem, ln_memkv, w_mk, w_mv):
    B, M, D = mem.shape
    depth = w_mk.shape[0]
    return pl.pallas_call(
        _memkv_kernel,
        grid=(depth, B),
        in_specs=[
            pl.BlockSpec((1, M, D), lambda l, b: (b, 0, 0)),
            pl.BlockSpec((1, 1, D), lambda l, b: (l, 0, 0)),
            pl.BlockSpec((1, D, D), lambda l, b: (l, 0, 0)),
            pl.BlockSpec((1, D, D), lambda l, b: (l, 0, 0)),
        ],
        out_specs=[
            pl.BlockSpec((1, 1, M, D), lambda l, b: (l, b, 0, 0)),
            pl.BlockSpec((1, 1, M, D), lambda l, b: (l, b, 0, 0)),
        ],
        out_shape=[jax.ShapeDtypeStruct((depth, B, M, D), BF16)] * 2,
        compiler_params=_params(("arbitrary", "arbitrary")),
        name="memkv",
    )(mem, ln_memkv.reshape(depth, 1, D), w_mk, w_mv)


def _proj_kernel(has_y, tiles_per_seq, *refs):
    if has_y:
        x_ref, y_ref = refs[:2]
        refs = refs[2:]
        x = x_ref[...] + y_ref[...].astype(F32)
    else:
        x_ref = refs[0]
        refs = refs[1:]
        x = x_ref[...]
    (g_ref, wml_ref, wsqt_ref, wsk_ref, wsvt_ref, wg_ref, wgt_ref, bg_ref, bgt_ref, cw_ref, cb_ref,
     ml_ref, sqt_ref, sk_ref, svt_ref, gc_ref, gt_ref, prev_ref) = refs
    tm, W, L, P = x.shape[0], ML_WIDTH, ML_CHUNK, SB_SPAN

    @pl.when(pl.program_id(0) % tiles_per_seq == 0)
    def _():
        prev_ref[...] = jnp.zeros_like(prev_ref)

    h = _rms(x, g_ref[0]).astype(BF16)
    qk = _dot(h, wml_ref[0, :, :2 * W])
    ext = jnp.concatenate([prev_ref[...], qk], axis=0)
    prev_ref[...] = qk[tm - 8:, :]
    cw = cw_ref[0]
    acc = cb_ref[0]
    for tap in range(CONV_W):
        lo = 8 - (CONV_W - 1) + tap
        acc = acc + ext[lo:lo + tm, :] * cw[tap:tap + 1, :]
    qa = acc * _sigmoid(acc)
    ml_ref[:, :W] = qa[:, :W].astype(BF16)
    ml_ref[:, W:2 * W] = (qa[:, W:] * (ML_DHEAD ** -0.5)).astype(BF16)
    ml_ref[:, 2 * W:] = _dot(h, wml_ref[0, :, 2 * W:]).astype(BF16)
    sqt = (_dot_nt(wsqt_ref[0], h) * (SB_DHEAD ** -0.5 * LOG2E)).astype(BF16)
    svt = _dot_nt(wsvt_ref[0], h).astype(BF16)
    for c in range(tm // P):
        sqt_ref[c] = sqt[:, c * P:(c + 1) * P]
        svt_ref[c] = svt[:, c * P:(c + 1) * P]
    sk_ref[...] = _dot(h, wsk_ref[0]).astype(BF16)
    gc_ref[...] = _dot(h, wg_ref[0]) + bg_ref[0]
    gt = _dot_nt(wgt_ref[0], h) + bgt_ref[0]
    for c in range(tm // L):
        gt_ref[c] = gt[:, c * L:(c + 1) * L]


def _proj(l, S, x2, y, ln_mix, w_ml, w_sqt, w_sk, w_svt, w_g, w_gt, b_g, b_gt, conv_w, conv_b):
    N, D = x2.shape
    tm, W, L, P, SW = TOKEN_TILE, ML_WIDTH, ML_CHUNK, SB_SPAN, SB_WIDTH
    row = lambda i: (i, 0)
    lay = lambda i: (l, 0, 0)
    ins = [x2] + ([y] if y is not None else [])
    in_specs = [pl.BlockSpec((tm, D), row)] * len(ins) + [
        pl.BlockSpec((1, 1, D), lay),
        pl.BlockSpec((1, D, 4 * W), lay),
        pl.BlockSpec((1, SW, D), lay),
        pl.BlockSpec((1, D, SW), lay),
        pl.BlockSpec((1, SW, D), lay),
        pl.BlockSpec((1, D, LANES), lay),
        pl.BlockSpec((1, 8, D), lay),
        pl.BlockSpec((1, 1, LANES), lay),
        pl.BlockSpec((1, 8, 1), lay),
        pl.BlockSpec((1, CONV_W, 2 * W), lay),
        pl.BlockSpec((1, 1, 2 * W), lay),
    ]
    return pl.pallas_call(
        functools.partial(_proj_kernel, y is not None, S // tm),
        grid=(N // tm,),
        in_specs=in_specs,
        out_specs=[
            pl.BlockSpec((tm, 4 * W), row),
            pl.BlockSpec((tm // P, SW, P), lambda i: (i, 0, 0)),
            pl.BlockSpec((tm, SW), row),
            pl.BlockSpec((tm // P, SW, P), lambda i: (i, 0, 0)),
            pl.BlockSpec((tm, LANES), row),
            pl.BlockSpec((tm // L, 8, L), lambda i: (i, 0, 0)),
        ],
        out_shape=[
            jax.ShapeDtypeStruct((N, 4 * W), BF16),
            jax.ShapeDtypeStruct((N // P, SW, P), BF16),
            jax.ShapeDtypeStruct((N, SW), BF16),
            jax.ShapeDtypeStruct((N // P, SW, P), BF16),
            jax.ShapeDtypeStruct((N, LANES), F32),
            jax.ShapeDtypeStruct((N // L, 8, L), F32),
        ],
        scratch_shapes=[pltpu.VMEM((8, 2 * W), F32)],
        compiler_params=_params(("arbitrary",)),
        name="proj",
    )(*ins, ln_mix, w_ml, w_sqt, w_sk, w_svt, w_g, w_gt, b_g, b_gt, conv_w, conv_b)


def _mlstm_kernel(ml_ref, gc_ref, gt_ref, hg_ref, out_ref, s_ref, m_ref):
    L, d, W, H = ML_CHUNK, ML_DHEAD, ML_WIDTH, ML_HEADS
    nb = ml_ref.shape[0]

    @pl.when(pl.program_id(1) == 0)
    def _():
        s_ref[...] = jnp.zeros_like(s_ref)
        m_ref[...] = jnp.zeros_like(m_ref)

    row = lax.broadcasted_iota(I32, (L, L), 0)
    col = lax.broadcasted_iota(I32, (L, L), 1)
    causal = col <= row
    tri = jnp.where(causal, 1.0, 0.0).astype(BF16)
    tri_t = jnp.where(row <= col, 1.0, 0.0).astype(BF16)
    lane = lax.broadcasted_iota(I32, (L, LANES), 1)
    ones_cols = jnp.ones((L, d), BF16)
    ones_sq = jnp.ones((d, d), BF16)
    r2 = lax.broadcasted_iota(I32, (LANES, 2 * d), 0)
    c2 = lax.broadcasted_iota(I32, (LANES, 2 * d), 1)
    pick = [jnp.where(((c2 < d) & (r2 == H + h)) | ((c2 >= d) & (r2 == h)), 1.0, 0.0).astype(BF16)
            for h in range(H)]
    hg = hg_ref[0]
    units = [(bb, h) for bb in range(nb) for h in range(H)]

    gts, bts, xs = [], [], []
    for bb in range(nb):
        gt = gt_ref[bb, 0]
        gc = gc_ref[bb]
        lsc_hi, lsc_lo = _split(-_softplus(-gc))
        bc = _dot(tri, lsc_hi) + _dot(tri, lsc_lo)
        lst_hi, lst_lo = _split(-_softplus(-gt))
        gts.append(gt)
        bts.append(_dot(lst_hi, tri_t) + _dot(lst_lo, tri_t))
        xs.append(_split(jnp.where(lane < H, gc, bc)))

    qbs, kbs, v1s, qks, carried, spread = [], [], [], [], [], []
    for bb, h in units:
        qb = ml_ref[bb, :, h * d:(h + 1) * d]
        kb = ml_ref[bb, :, W + h * d:W + (h + 1) * d]
        v = ml_ref[bb, :, 2 * W + h * d:2 * W + (h + 1) * d]
        qbs.append(qb)
        kbs.append(kb)
        v1s.append(jnp.concatenate([v, ones_cols], axis=1))
        qks.append(_dot_nt(qb, kb))
        carried.append(_dot(qb, s_ref[bb * H + h].astype(BF16)))
        spread.append(_dot(xs[bb][0], pick[h]) + _dot(xs[bb][1], pick[h]))

    ss, m_ts, w_inters = [], [], []
    for u, (bb, h) in enumerate(units):
        b_c = spread[u][:, :d]
        i_r = gts[bb][h:h + 1, :]
        b_r = bts[bb][H + h:H + h + 1, :]
        logd = jnp.where(causal, b_c - b_r + i_r, -jnp.inf)
        inter = b_c + m_ref[u][:1, :]
        m_t = jnp.maximum(inter, jnp.max(logd, axis=-1, keepdims=True))
        ss.append((qks[u] * jnp.exp(logd - m_t)).astype(BF16))
        m_ts.append(m_t)
        w_inters.append(jnp.exp(inter - m_t))

    intras = [_dot(ss[u], v1s[u]) for u in range(len(units))]

    hhs = []
    for u in range(len(units)):
        w2 = jnp.concatenate([w_inters[u], w_inters[u]], axis=1)
        tot = intras[u] + w2 * carried[u]
        hhs.append(tot[:, :d] / jnp.maximum(jnp.abs(tot[:, d:]), jnp.exp(-m_ts[u])))
    sqs = [_dot((hh * hh).astype(BF16), ones_sq) * (1.0 / d) for hh in hhs]
    for u, (bb, h) in enumerate(units):
        og = ml_ref[bb, :, 3 * W + h * d:3 * W + (h + 1) * d].astype(F32)
        hn = hhs[u] * lax.rsqrt(sqs[u] + EPS) * hg[:, h * d:(h + 1) * d]
        out_ref[bb, :, h * d:(h + 1) * d] = (_sigmoid(og) * hn).astype(BF16)

    kws, w_olds, m_news = [], [], []
    for u, (bb, h) in enumerate(units):
        b_c, i_c = spread[u][:, :d], spread[u][:, d:]
        i_r = gts[bb][h:h + 1, :]
        b_r = bts[bb][H + h:H + h + 1, :]
        m_prev = m_ref[u][:1, :1]
        b_last = b_r[:, L - 1:L]
        m_new = jnp.maximum(b_last + m_prev, jnp.max(b_last - b_r + i_r, axis=-1, keepdims=True))
        w_olds.append(jnp.exp(b_last + m_prev - m_new))
        m_news.append(m_new)
        kws.append((kbs[u].astype(F32) * jnp.exp(b_last - b_c + i_c - m_new)).T.astype(BF16))
    for u in range(len(units)):
        s_ref[u] = w_olds[u] * s_ref[u] + _dot(kws[u], v1s[u])
        m_ref[u] = jnp.broadcast_to(m_news[u], (8, LANES))


def _mlstm(l, B, S, ml, gc, gt, ml_head_g):
    L, W, H, d = ML_CHUNK, ML_WIDTH, ML_HEADS, ML_DHEAD
    nc = S // L
    nb = 4 if B % 4 == 0 else (2 if B % 2 == 0 else 1)
    out = pl.pallas_call(
        _mlstm_kernel,
        grid=(B // nb, nc),
        in_specs=[
            pl.BlockSpec((nb, L, 4 * W), lambda b, c: (b, c, 0)),
            pl.BlockSpec((nb, L, LANES), lambda b, c: (b, c, 0)),
            pl.BlockSpec((nb, 1, 8, L), lambda b, c: (b, c, 0, 0)),
            pl.BlockSpec((1, 1, W), lambda b, c: (l, 0, 0)),
        ],
        out_specs=pl.BlockSpec((nb, L, W), lambda b, c: (b, c, 0)),
        out_shape=jax.ShapeDtypeStruct((B, S, W), BF16),
        scratch_shapes=[
            pltpu.VMEM((nb * H, d, 2 * d), F32),
            pltpu.VMEM((nb * H, 8, LANES), F32),
        ],
        compiler_params=_params(("arbitrary", "arbitrary")),
        name="mlstm",
    )(ml.reshape(B, S, 4 * W), gc.reshape(B, S, LANES), gt.reshape(B, nc, 8, L), ml_head_g)
    return out.reshape(B * S, W)


def _sb_kernel(qt_ref, k_ref, vt_ref, g_ref, o_ref, acc_ref):
    P, H, d = SB_SPAN, SB_HEADS, SB_DHEAD
    i = pl.program_id(1)
    row = lax.broadcasted_iota(I32, (P, P), 0)
    col = lax.broadcasted_iota(I32, (P, P), 1)
    strict = row < col
    neg_from = jnp.where(col >= row, -1.0, 0.0).astype(BF16)

    def span(j, csums, diagonal):
        start = pl.multiple_of(j * P, P)
        heads = [slice(h * d, (h + 1) * d) for h in range(H)]
        ks = [k_ref[pl.ds(start, P), hs] for hs in heads]
        qts = [qt_ref[0, hs, :] for hs in heads]
        zs = [_dot(kj, qt) for kj, qt in zip(ks, qts)]
        sps = []
        for z in zs:
            sp = jnp.maximum(z, 0.0) + jnp.log(1.0 + jnp.exp2(-jnp.abs(z))) * LOG2E
            sps.append(jnp.where(strict, sp, 0.0) if diagonal else sp)
        es = [_dot(jnp.concatenate([kj, neg_from], axis=1), jnp.concatenate([qt, sp.astype(BF16)], axis=0))
              for kj, qt, sp in zip(ks, qts, sps)]
        out = []
        for h, hs in enumerate(heads):
            csum = jnp.sum(sps[h], axis=0, keepdims=True)
            if diagonal:
                a = jnp.where(strict, jnp.exp2(es[h]), 0.0)
                acc_ref[hs, :] = _dot(vt_ref[j, hs, :], a.astype(BF16))
                out.append(csum)
            else:
                a = jnp.exp2(es[h] - csums[h])
                acc_ref[hs, :] += _dot(vt_ref[j, hs, :], a.astype(BF16))
                out.append(csums[h] + csum)
        return tuple(out)

    csums = span(i, None, True)
    lax.fori_loop(0, i, lambda jj, c: span(i - 1 - jj, c, False), csums)
    g = g_ref[0]
    for h in range(H):
        hs = slice(h * d, (h + 1) * d)
        acc = acc_ref[hs, :]
        on = acc * lax.rsqrt(jnp.mean(acc * acc, axis=0, keepdims=True) + EPS)
        o_ref[:, hs] = (on.T * g[:, hs]).astype(BF16)


def _stickbreak(l, B, S, sqt, sk, svt, sb_head_g):
    P, W = SB_SPAN, SB_WIDTH
    nq = S // P
    return pl.pallas_call(
        _sb_kernel,
        grid=(B, nq),
        in_specs=[
            pl.BlockSpec((1, W, P), lambda b, i: (b * nq + i, 0, 0)),
            pl.BlockSpec((S, W), lambda b, i: (b, 0)),
            pl.BlockSpec((nq, W, P), lambda b, i: (b, 0, 0)),
            pl.BlockSpec((1, 1, W), lambda b, i: (l, 0, 0)),
        ],
        out_specs=pl.BlockSpec((P, W), lambda b, i: (b * nq + i, 0)),
        out_shape=jax.ShapeDtypeStruct((B * S, W), BF16),
        scratch_shapes=[pltpu.VMEM((W, P), F32)],
        compiler_params=_params(("arbitrary", "arbitrary")),
        name="stickbreak",
    )(sqt, sk, svt, sb_head_g)


def _route(lgt, tm):
    G, E = N_GROUPS, EXP_PER_GROUP
    lg = [lgt[i:i + 1, :] for i in range(G + G * E)]
    mg = functools.reduce(jnp.maximum, lg[:G])
    eg = [jnp.exp(v - mg) for v in lg[:G]]
    zg = functools.reduce(jnp.add, eg)
    pg = [v / zg for v in eg]
    p_sel = functools.reduce(jnp.maximum, pg)
    gid = jnp.where(pg[0] == p_sel, 0, jnp.where(pg[1] == p_sel, 1, jnp.where(pg[2] == p_sel, 2, 3)))
    es = [jnp.where(gid == 0, lg[G + e], jnp.where(gid == 1, lg[G + E + e],
          jnp.where(gid == 2, lg[G + 2 * E + e], lg[G + 3 * E + e]))) for e in range(E)]
    me = functools.reduce(jnp.maximum, es)
    ee = [jnp.exp(v - me) for v in es]
    ze = functools.reduce(jnp.add, ee)
    pe = [v / ze for v in ee]

    def first_max(vals):
        top = functools.reduce(jnp.maximum, vals)
        idx = jnp.where(vals[0] == top, 0, jnp.where(vals[1] == top, 1, jnp.where(vals[2] == top, 2, 3)))
        return top, idx

    v1, i1 = first_max(pe)
    v2, i2 = first_max([jnp.where(i1 == e, -1.0, pe[e]) for e in range(E)])
    tsum = v1 + v2
    tw1, tw2 = v1 / tsum, v2 / tsum
    gates = [p_sel * (jnp.where(i1 == e, tw1, 0.0) + jnp.where(i2 == e, tw2, 0.0)) for e in range(E)]

    sub8 = lax.broadcasted_iota(I32, (8, tm), 0)
    onehot = jnp.where(sub8 == gid, 1.0, 0.0)
    r = lax.broadcasted_iota(I32, (SORT_BLOCK, SORT_BLOCK), 0)
    c = lax.broadcasted_iota(I32, (SORT_BLOCK, SORT_BLOCK), 1)
    before = jnp.where(r < c, 1.0, 0.0).astype(BF16)
    lane = lax.broadcasted_iota(I32, (8, LANES), 1)
    counts = jnp.zeros((8, LANES), F32)
    lps = []
    for jj in range(tm // SORT_BLOCK):
        oj = onehot[:, jj * SORT_BLOCK:(jj + 1) * SORT_BLOCK]
        pre = _dot(oj.astype(BF16), before)
        cnt = jnp.sum(oj, axis=-1, keepdims=True)
        pcnt = jnp.floor((cnt + (ROW_TILE_BF16 - 1.0)) * (1.0 / ROW_TILE_BF16)) * ROW_TILE_BF16
        start = jnp.zeros((1, 1), F32)
        lp = jnp.zeros((1, SORT_BLOCK), F32)
        for g in range(G):
            lp = lp + oj[g:g + 1, :] * (pre[g:g + 1, :] + start)
            start = start + pcnt[g:g + 1, :]
        lps.append(lp)
        counts = counts + jnp.where(lane == jj, cnt, 0.0)
    lp = jnp.concatenate(lps, axis=1)
    rows = gates + [gid.astype(F32), lp]
    pack = jnp.zeros((8, tm), F32)
    for kk, v in enumerate(rows):
        pack = pack + jnp.where(sub8 == kk, v, 0.0)
    return pack, counts.astype(I32)


def _post_kernel(has_y, *refs):
    if has_y:
        x_ref, y_ref = refs[:2]
        refs = refs[2:]
        x = x_ref[...] + y_ref[...].astype(F32)
    else:
        x_ref = refs[0]
        refs = refs[1:]
        x = x_ref[...]
    (hml_ref, hsb_ref, wo_ref, gmem_ref, wq_ref, k_ref, v_ref, wmo_ref, gffn_ref, wrh_ref, wrl_ref, br_ref,
     x2_ref, h3_ref, rt_ref, gcol_ref, cnt_ref) = refs
    tm = x.shape[0]
    x1 = x + _dot(hml_ref[...], wo_ref[0, :ML_WIDTH, :]) + _dot(hsb_ref[...], wo_ref[0, ML_WIDTH:, :])
    q = _dot(_rms(x1, gmem_ref[0]).astype(BF16), wq_ref[0]).astype(BF16)
    kk = k_ref[0, 0]
    vv = v_ref[0, 0]
    dh = q.shape[1] // XA_HEADS
    outs = []
    for hd in range(XA_HEADS):
        sl = slice(hd * dh, (hd + 1) * dh)
        s = _dot_nt(q[:, sl], kk[:, sl]) * (dh ** -0.5)
        p = jnp.exp(s - jnp.max(s, axis=-1, keepdims=True))
        p = p / jnp.sum(p, axis=-1, keepdims=True)
        outs.append(_dot(p.astype(BF16), vv[:, sl]).astype(BF16))
    x2 = x1 + _dot(jnp.concatenate(outs, axis=-1), wmo_ref[0])
    x2_ref[...] = x2
    h3_hi, h3_lo = _split(_rms(x2, gffn_ref[0]))
    h3_ref[...] = h3_hi
    wrh = wrh_ref[0]
    lg = _dot(h3_hi, wrh) + _dot(h3_lo, wrh) + _dot(h3_hi, wrl_ref[0]) + br_ref[0]
    pack, counts = _route(lg.T, tm)
    rt_ref[...] = pack
    gcol_ref[...] = jnp.concatenate([pack, jnp.zeros((LANES - 8, tm), F32)], axis=0).T
    cnt_ref[0] = counts


def _post(l, B, S, x2, y, hml, hsb, w_out, ln_mem, w_mq, kx, vx, w_mo, ln_ffn, wr_hi, wr_lo, b_r):
    N, D = x2.shape
    M = kx.shape[2]
    tm = TOKEN_TILE
    tpb = S // tm
    row = lambda i: (i, 0)
    lay = lambda i: (l, 0, 0)
    ins = [x2] + ([y] if y is not None else [])
    in_specs = [pl.BlockSpec((tm, D), row)] * len(ins) + [
        pl.BlockSpec((tm, ML_WIDTH), row),
        pl.BlockSpec((tm, SB_WIDTH), row),
        pl.BlockSpec((1, D, D), lay),
        pl.BlockSpec((1, 1, D), lay),
        pl.BlockSpec((1, D, D), lay),
        pl.BlockSpec((1, 1, M, D), lambda i: (l, i // tpb, 0, 0)),
        pl.BlockSpec((1, 1, M, D), lambda i: (l, i // tpb, 0, 0)),
        pl.BlockSpec((1, D, D), lay),
        pl.BlockSpec((1, 1, D), lay),
        pl.BlockSpec((1, D, LANES), lay),
        pl.BlockSpec((1, D, LANES), lay),
        pl.BlockSpec((1, 1, LANES), lay),
    ]
    return pl.pallas_call(
        functools.partial(_post_kernel, y is not None),
        grid=(N // tm,),
        in_specs=in_specs,
        out_specs=[
            pl.BlockSpec((tm, D), row),
            pl.BlockSpec((tm, D), row),
            pl.BlockSpec((8, tm), lambda i: (0, i)),
            pl.BlockSpec((tm, LANES), row),
            pl.BlockSpec((1, 8, LANES), lambda i: (i, 0, 0)),
        ],
        out_shape=[
            jax.ShapeDtypeStruct((N, D), F32),
            jax.ShapeDtypeStruct((N, D), BF16),
            jax.ShapeDtypeStruct((8, N), F32),
            jax.ShapeDtypeStruct((N, LANES), F32),
            jax.ShapeDtypeStruct((N // tm, 8, LANES), I32),
        ],
        compiler_params=_params(("arbitrary",)),
        name="post",
    )(*ins, hml, hsb, w_out, ln_mem, w_mq, kx, vx, w_mo, ln_ffn, wr_hi, wr_lo, b_r)


def _moe_kernel(T, cnt_ref, h3_ref, rt_ref, gcol_ref, w1_ref, w3_ref, w2_ref, y_ref,
                xs_ref, gs_ref, ys_ref, sl_ref, gl_ref):
    G, CH, SBK, WIN, RT = N_GROUPS, MOE_CHUNK, SORT_BLOCK, SORT_WIN, ROW_TILE_BF16
    b, g, hf = pl.program_id(0), pl.program_id(1), pl.program_id(2)
    nsb = T // SBK
    n_half = pl.num_programs(2)
    D = h3_ref.shape[1]

    pc = [[((cnt_ref[(b * nsb + j) * G + gg] + (RT - 1)) // RT) * RT for j in range(nsb)] for gg in range(G)]
    base = [0]
    for gg in range(G):
        base.append(base[-1] + functools.reduce(lambda u, w: u + w, pc[gg]))

    def seg_starts(j):
        out, src = [], 0
        for gg in range(G):
            dst = base[gg]
            for jp in range(j):
                dst = dst + pc[gg][jp]
            out.append((src, dst))
            src = src + pc[gg][j]
        return out

    @pl.when((g == 0) & (hf == 0))
    def _sort():
        total = pl.multiple_of(base[G], RT)
        xs_ref[pl.ds(total, CH), :] = jnp.zeros((CH, D), BF16)
        gs_ref[pl.ds(total, CH), :] = jnp.zeros((CH, LANES), F32)
        riota = lax.broadcasted_iota(I32, (WIN, SBK), 0)
        for j in range(nsb):
            lp = rt_ref[5:6, j * SBK:(j + 1) * SBK].astype(I32)
            perm = jnp.where(riota == lp, 1.0, 0.0).astype(BF16)
            sl_ref[...] = _dot(perm, h3_ref[j * SBK:(j + 1) * SBK, :]).astype(BF16)
            g_hi, g_lo = _split(gcol_ref[j * SBK:(j + 1) * SBK, :])
            gl_ref[...] = _dot(perm, g_hi) + _dot(perm, g_lo)
            for gg, (src, dst) in enumerate(seg_starts(j)):
                def copy_in(c, carry, src=src, dst=dst):
                    s = pl.multiple_of(src + c * RT, RT)
                    t = pl.multiple_of(dst + c * RT, RT)
                    xs_ref[pl.ds(t, RT), :] = sl_ref[pl.ds(s, RT), :]
                    gs_ref[pl.ds(t, RT), :] = gl_ref[pl.ds(s, RT), :]
                    return carry
                lax.fori_loop(0, pc[gg][j] // RT, copy_in, 0)

    rows_g = base[1] - base[0]
    base_g = base[0]
    for gg in range(1, G):
        rows_g = jnp.where(g == gg, base[gg + 1] - base[gg], rows_g)
        base_g = jnp.where(g == gg, base[gg], base_g)

    def chunk(c, carry):
        r0 = pl.multiple_of(base_g + c * CH, RT)
        xc = xs_ref[pl.ds(r0, CH), :]
        gc = gs_ref[pl.ds(r0, CH), :]
        acc = None
        for e in range(EXPERTS_PER_STEP):
            a = _dot(xc, w1_ref[e])
            u = _dot(xc, w3_ref[e])
            gate = gc[:, e:e + 1]
            for k in range(1, EXP_PER_GROUP // EXPERTS_PER_STEP):
                gate = jnp.where(hf == k, gc[:, k * EXPERTS_PER_STEP + e:k * EXPERTS_PER_STEP + e + 1], gate)
            hm = (a * _sigmoid(a)) * u * gate
            part = _dot(hm.astype(BF16), w2_ref[e])
            acc = part if acc is None else acc + part

        @pl.when(hf == 0)
        def _():
            ys_ref[pl.ds(r0, CH), :] = acc

        @pl.when(hf != 0)
        def _():
            ys_ref[pl.ds(r0, CH), :] += acc

        return carry

    lax.fori_loop(0, (rows_g + (CH - 1)) // CH, chunk, 0)

    @pl.when((g == G - 1) & (hf == n_half - 1))
    def _unsort():
        ciota = lax.broadcasted_iota(I32, (SBK, WIN), 1)
        for j in range(nsb):
            for gg, (src, dst) in enumerate(seg_starts(j)):
                def copy_out(c, carry, src=src, dst=dst):
                    s = pl.multiple_of(src + c * RT, RT)
                    t = pl.multiple_of(dst + c * RT, RT)
                    sl_ref[pl.ds(s, RT), :] = ys_ref[pl.ds(t, RT), :].astype(BF16)
                    return carry
                lax.fori_loop(0, pc[gg][j] // RT, copy_out, 0)
            lp = gcol_ref[j * SBK:(j + 1) * SBK, 5:6].astype(I32)
            perm_t = jnp.where(ciota == lp, 1.0, 0.0).astype(BF16)
            y_ref[j * SBK:(j + 1) * SBK, :] = _dot(perm_t, sl_ref[...]).astype(BF16)


def _moe(l, T, counts, h3, rt, gcol, w1, w3, w2):
    N, D = h3.shape
    G, E, EPS_ = N_GROUPS, EXP_PER_GROUP, EXPERTS_PER_STEP
    F = w1.shape[-1]
    rows = T + (T // SORT_BLOCK) * G * (ROW_TILE_BF16 - 1) + MOE_CHUNK
    rows = -(-rows // ROW_TILE_BF16) * ROW_TILE_BF16
    grid_spec = pltpu.PrefetchScalarGridSpec(
        num_scalar_prefetch=1,
        grid=(N // T, G, E // EPS_),
        in_specs=[
            pl.BlockSpec((T, D), lambda b, g, hf, cnt: (b, 0)),
            pl.BlockSpec((8, T), lambda b, g, hf, cnt: (0, b)),
            pl.BlockSpec((T, LANES), lambda b, g, hf, cnt: (b, 0)),
            pl.BlockSpec((None, None, EPS_, D, F), lambda b, g, hf, cnt: (l, g, hf, 0, 0)),
            pl.BlockSpec((None, None, EPS_, D, F), lambda b, g, hf, cnt: (l, g, hf, 0, 0)),
            pl.BlockSpec((None, None, EPS_, F, D), lambda b, g, hf, cnt: (l, g, hf, 0, 0)),
        ],
        out_specs=pl.BlockSpec((T, D), lambda b, g, hf, cnt: (b, 0)),
        scratch_shapes=[
            pltpu.VMEM((rows, D), BF16),
            pltpu.VMEM((rows, LANES), F32),
            pltpu.VMEM((rows, D), F32),
            pltpu.VMEM((SORT_WIN, D), BF16),
            pltpu.VMEM((SORT_WIN, LANES), F32),
        ],
    )
    return pl.pallas_call(
        functools.partial(_moe_kernel, T),
        grid_spec=grid_spec,
        out_shape=jax.ShapeDtypeStruct((N, D), BF16),
        compiler_params=_params(("arbitrary", "arbitrary", "arbitrary")),
        name="moe",
    )(counts, h3, rt, gcol, w1, w3, w2)


def _final_kernel(x_ref, y_ref, g_ref, o_ref):
    o_ref[...] = _rms(x_ref[...] + y_ref[...].astype(F32), g_ref[...])


def _final(x2, y, ln_final):
    N, D = x2.shape
    tm = TOKEN_TILE
    return pl.pallas_call(
        _final_kernel,
        grid=(N // tm,),
        in_specs=[pl.BlockSpec((tm, D), lambda i: (i, 0)), pl.BlockSpec((tm, D), lambda i: (i, 0)),
                  pl.BlockSpec((1, D), lambda i: (0, 0))],
        out_specs=pl.BlockSpec((tm, D), lambda i: (i, 0)),
        out_shape=jax.ShapeDtypeStruct((N, D), F32),
        compiler_params=_params(("arbitrary",)),
        name="final_norm",
    )(x2, y, ln_final.reshape(1, D))


def kernel(x, mem, ln_mix, w_in, conv_w, conv_b, i_bias, f_bias, ml_head_g, sb_head_g, w_out, ln_mem, ln_memkv,
           w_mq, w_mk, w_mv, w_mo, ln_ffn, w_rg, b_rg, w_re, b_re, w_e1, w_e3, w_e2, ln_final):
    B, S, D = x.shape
    N = B * S
    depth = w_in.shape[0]
    G, E = N_GROUPS, EXP_PER_GROUP
    assert S % TOKEN_TILE == 0 and TOKEN_TILE % ML_CHUNK == 0 and TOKEN_TILE % SORT_BLOCK == 0 and S % SB_SPAN == 0
    T = 2048 if S % 2048 == 0 else S

    c0, c1 = 4 * ML_WIDTH, 4 * ML_WIDTH + 2 * ML_HEADS
    w_ml = w_in[:, :, :c0].astype(BF16)
    w_sqt = jnp.swapaxes(w_in[:, :, c1:c1 + SB_WIDTH], 1, 2).astype(BF16)
    w_sk = w_in[:, :, c1 + SB_WIDTH:c1 + 2 * SB_WIDTH].astype(BF16)
    w_svt = jnp.swapaxes(w_in[:, :, c1 + 2 * SB_WIDTH:], 1, 2).astype(BF16)
    w_gate = w_in[:, :, c0:c1]
    w_g = jnp.pad(w_gate, ((0, 0), (0, 0), (0, LANES - 8))).astype(BF16)
    w_gt = jnp.swapaxes(w_gate, 1, 2).astype(BF16)
    gate_b = jnp.concatenate([i_bias, f_bias], axis=-1)
    b_g = jnp.pad(gate_b, ((0, 0), (0, LANES - 8))).reshape(depth, 1, LANES)
    b_gt = gate_b.reshape(depth, 8, 1)
    w_r = jnp.pad(jnp.concatenate([w_rg, w_re], axis=-1), ((0, 0), (0, 0), (0, LANES - G - G * E)))
    wr_hi = w_r.astype(BF16)
    wr_lo = (w_r - wr_hi.astype(F32)).astype(BF16)
    b_r = jnp.pad(jnp.concatenate([b_rg, b_re], axis=-1), ((0, 0), (0, LANES - G - G * E))).reshape(depth, 1, LANES)
    w_out_b, w_mq_b, w_mo_b = w_out.astype(BF16), w_mq.astype(BF16), w_mo.astype(BF16)
    w1, w3, w2 = w_e1.astype(BF16), w_e3.astype(BF16), w_e2.astype(BF16)
    r3 = lambda a: a.reshape(depth, 1, a.shape[-1])

    kx, vx = _memkv(mem, ln_memkv, w_mk.astype(BF16), w_mv.astype(BF16))

    x2, y = x.reshape(N, D), None
    nsub = TOKEN_TILE // SORT_BLOCK
    for l in range(depth):
        ml, sqt, sk, svt, gc, gt = _proj(l, S, x2, y, r3(ln_mix), w_ml, w_sqt, w_sk, w_svt, w_g, w_gt, b_g, b_gt,
                                         conv_w, r3(conv_b))
        hml = _mlstm(l, B, S, ml, gc, gt, r3(ml_head_g))
        hsb = _stickbreak(l, B, S, sqt, sk, svt, r3(sb_head_g))
        x2, h3, rt, gcol, cnt = _post(l, B, S, x2, y, hml, hsb, w_out_b, r3(ln_mem), w_mq_b, kx, vx, w_mo_b,
                                      r3(ln_ffn), wr_hi, wr_lo, b_r)
        counts = jnp.swapaxes(cnt[:, :G, :nsub], 1, 2).reshape(-1)
        y = _moe(l, T, counts, h3, rt, gcol, w1, w3, w2)
    return _final(x2, y, ln_final).reshape(B, S, D)
```

```python
import functools

import jax
import jax.numpy as jnp
from jax import lax
from jax.experimental import pallas as pl
from jax.experimental.pallas import tpu as pltpu

F32 = jnp.float32
BF16 = jnp.bfloat16
I32 = jnp.int32

ML_HEADS = 4
ML_DHEAD = 128
SB_HEADS = 4
SB_DHEAD = 128
ML_WIDTH = ML_HEADS * ML_DHEAD
SB_WIDTH = SB_HEADS * SB_DHEAD
CONV_W = 4
ML_CHUNK = 128
SB_SPAN = 256
XA_HEADS = 4
N_GROUPS = 4
EXP_PER_GROUP = 4
EPS = 1e-6
LOG2E = 1.4426950408889634

LANES = 128
ROW_TILE_BF16 = 16
SORT_BLOCK = 256
SORT_WIN = 384
MOE_CHUNK = 256
MOE_CHUNK_TAIL = 128
EXPERTS_PER_STEP = 2
TOKEN_TILE = 512
VMEM_LIMIT = 56 * 1024 * 1024


def _dot(a, b):
    return jnp.dot(a, b, preferred_element_type=F32)


def _dot_nt(a, b):
    return lax.dot_general(a, b, (((1,), (1,)), ((), ())), preferred_element_type=F32)


def _split(x):
    hi = x.astype(BF16)
    lo = (x - hi.astype(F32)).astype(BF16)
    return hi, lo


def _rms(x, g):
    return x * lax.rsqrt(jnp.mean(x * x, axis=-1, keepdims=True) + EPS) * g


def _sigmoid(x):
    return 1.0 / (1.0 + jnp.exp(-x))


def _softplus(x):
    return jnp.maximum(x, 0.0) + jnp.log1p(jnp.exp(-jnp.abs(x)))


def _params(sem):
    return pltpu.CompilerParams(dimension_semantics=sem, vmem_limit_bytes=VMEM_LIMIT)


def _memkv_kernel(mem_ref, g_ref, wk_ref, wv_ref, k_ref, v_ref):
    h = _rms(mem_ref[0], g_ref[0]).astype(BF16)
    k_ref[0, 0] = _dot(h, wk_ref[0]).astype(BF16)
    v_ref[0, 0] = _dot(h, wv_ref[0]).astype(BF16)


def _memkv(mem, ln_memkv, w_mk, w_mv):
    B, M, D = mem.shape
    depth = w_mk.shape[0]
    return pl.pallas_call(
        _memkv_kernel,
        grid=(depth, B),
        in_specs=[
            pl.BlockSpec((1, M, D), lambda l, b: (b, 0, 0)),
            pl.BlockSpec((1, 1, D), lambda l, b: (l, 0, 0)),
            pl.BlockSpec((1, D, D), lambda l, b: (l, 0, 0)),
            pl.BlockSpec((1, D, D), lambda l, b: (l, 0, 0)),
        ],
        out_specs=[
            pl.BlockSpec((1, 1, M, D), lambda l, b: (l, b, 0, 0)),
            pl.BlockSpec((1, 1, M, D), lambda l, b: (l, b, 0, 0)),
        ],
        out_shape=[jax.ShapeDtypeStruct((depth, B, M, D), BF16)] * 2,
        compiler_params=_params(("arbitrary", "arbitrary")),
        name="memkv",
    )(mem, ln_memkv.reshape(depth, 1, D), w_mk, w_mv)


def _proj_kernel(has_y, tiles_per_seq, *refs):
    if has_y:
        x_ref, y_ref = refs[:2]
        refs = refs[2:]
        x = x_ref[...] + y_ref[...].astype(F32)
    else:
        x_ref = refs[0]
        refs = refs[1:]
        x = x_ref[...]
    (g_ref, wml_ref, wsqt_ref, wsk_ref, wsvt_ref, wg_ref, wgt_ref, bg_ref, bgt_ref, cw_ref, cb_ref,
     qka_ref, vo_ref, sqt_ref, sk_ref, svt_ref, gc_ref, gt_ref, qk_ref) = refs
    tm, W, L, P = x.shape[0], ML_WIDTH, ML_CHUNK, SB_SPAN

    @pl.when(pl.program_id(0) % tiles_per_seq == 0)
    def _():
        qk_ref[:8, :] = jnp.zeros((8, 2 * W), F32)

    h = _rms(x, g_ref[0]).astype(BF16)
    cw = cw_ref[0]
    cb = cb_ref[0]
    cblk = 2 * LANES

    def conv_block(c):
        rows = L
        for r in range(0, tm, rows):
            acc = cb[:, c:c + cblk]
            for tap in range(CONV_W):
                lo = 8 - (CONV_W - 1) + tap + r
                acc = acc + qk_ref[lo:lo + rows, c:c + cblk] * cw[tap:tap + 1, c:c + cblk]
            qa = acc * _sigmoid(acc)
            if c >= W:
                qa = qa * (ML_DHEAD ** -0.5)
            qka_ref[r:r + rows, c:c + cblk] = qa.astype(BF16)
        qk_ref[:8, c:c + cblk] = qk_ref[tm:, c:c + cblk]

    def store_v():
        vo_ref[:, :W] = _dot(h, wml_ref[0, :, 2 * W:3 * W]).astype(BF16)

    def store_o():
        vo_ref[:, W:] = _dot(h, wml_ref[0, :, 3 * W:]).astype(BF16)

    def store_sk():
        sk_ref[...] = _dot(h, wsk_ref[0]).astype(BF16)

    def store_sqt():
        sqt = (_dot_nt(wsqt_ref[0], h) * (SB_DHEAD ** -0.5 * LOG2E)).astype(BF16)
        for c in range(tm // P):
            sqt_ref[c] = sqt[:, c * P:(c + 1) * P]

    def store_svt():
        svt = _dot_nt(wsvt_ref[0], h).astype(BF16)
        for c in range(tm // P):
            svt_ref[c] = svt[:, c * P:(c + 1) * P]

    others = [store_v, store_o, store_sk, store_sqt, store_svt]
    blocks = list(range(0, 2 * W, cblk))
    for n, c in enumerate(blocks):
        qk_ref[8:, c:c + cblk] = _dot(h, wml_ref[0, :, c:c + cblk])
        if n > 0:
            conv_block(blocks[n - 1])
        if others:
            others.pop(0)()
    conv_block(blocks[-1])
    for rest in others:
        rest()
    gc_ref[...] = _dot(h, wg_ref[0]) + bg_ref[0]
    gt = _dot_nt(wgt_ref[0], h) + bgt_ref[0]
    for c in range(tm // L):
        gt_ref[c] = gt[:, c * L:(c + 1) * L]


def _proj(l, S, x2, y, ln_mix, w_ml, w_sqt, w_sk, w_svt, w_g, w_gt, b_g, b_gt, conv_w, conv_b):
    N, D = x2.shape
    tm, W, L, P, SW = TOKEN_TILE, ML_WIDTH, ML_CHUNK, SB_SPAN, SB_WIDTH
    row = lambda i: (i, 0)
    lay = lambda i: (l, 0, 0)
    ins = [x2] + ([y] if y is not None else [])
    in_specs = [pl.BlockSpec((tm, D), row)] * len(ins) + [
        pl.BlockSpec((1, 1, D), lay),
        pl.BlockSpec((1, D, 4 * W), lay),
        pl.BlockSpec((1, SW, D), lay),
        pl.BlockSpec((1, D, SW), lay),
        pl.BlockSpec((1, SW, D), lay),
        pl.BlockSpec((1, D, LANES), lay),
        pl.BlockSpec((1, 8, D), lay),
        pl.BlockSpec((1, 1, LANES), lay),
        pl.BlockSpec((1, 8, 1), lay),
        pl.BlockSpec((1, CONV_W, 2 * W), lay),
        pl.BlockSpec((1, 1, 2 * W), lay),
    ]
    return pl.pallas_call(
        functools.partial(_proj_kernel, y is not None, S // tm),
        grid=(N // tm,),
        in_specs=in_specs,
        out_specs=[
            pl.BlockSpec((tm, 2 * W), row),
            pl.BlockSpec((tm, 2 * W), row),
            pl.BlockSpec((tm // P, SW, P), lambda i: (i, 0, 0)),
            pl.BlockSpec((tm, SW), row),
            pl.BlockSpec((tm // P, SW, P), lambda i: (i, 0, 0)),
            pl.BlockSpec((tm, LANES), row),
            pl.BlockSpec((tm // L, 8, L), lambda i: (i, 0, 0)),
        ],
        out_shape=[
            jax.ShapeDtypeStruct((N, 2 * W), BF16),
            jax.ShapeDtypeStruct((N, 2 * W), BF16),
            jax.ShapeDtypeStruct((N // P, SW, P), BF16),
            jax.ShapeDtypeStruct((N, SW), BF16),
            jax.ShapeDtypeStruct((N // P, SW, P), BF16),
            jax.ShapeDtypeStruct((N, LANES), F32),
            jax.ShapeDtypeStruct((N // L, 8, L), F32),
        ],
        scratch_shapes=[pltpu.VMEM((8 + tm, 2 * W), F32)],
        compiler_params=_params(("arbitrary",)),
        name="proj",
    )(*ins, ln_mix, w_ml, w_sqt, w_sk, w_svt, w_g, w_gt, b_g, b_gt, conv_w, conv_b)


def _mlstm_kernel(qk_ref, vo_ref, gc_ref, gt_ref, hg_ref, out_ref, s_ref, m_ref):
    L, d, W, H = ML_CHUNK, ML_DHEAD, ML_WIDTH, ML_HEADS
    nb = qk_ref.shape[0]

    @pl.when(pl.program_id(1) == 0)
    def _():
        s_ref[...] = jnp.zeros_like(s_ref)
        m_ref[...] = jnp.zeros_like(m_ref)

    row = lax.broadcasted_iota(I32, (L, L), 0)
    col = lax.broadcasted_iota(I32, (L, L), 1)
    causal = col <= row
    tri = jnp.where(causal, 1.0, 0.0).astype(BF16)
    tri_t = jnp.where(row <= col, 1.0, 0.0).astype(BF16)
    lane = lax.broadcasted_iota(I32, (L, LANES), 1)
    ones_cols = jnp.ones((L, d), BF16)
    ones_sq = jnp.ones((d, d), BF16)
    r2 = lax.broadcasted_iota(I32, (LANES, 2 * d), 0)
    c2 = lax.broadcasted_iota(I32, (LANES, 2 * d), 1)
    pick = [jnp.where(((c2 < d) & (r2 == H + h)) | ((c2 >= d) & (r2 == h)), 1.0, 0.0).astype(BF16)
            for h in range(H)]
    hg = hg_ref[0]
    units = [(bb, h) for bb in range(nb) for h in range(H)]

    gts, bts, xs = [], [], []
    for bb in range(nb):
        gt = gt_ref[bb, 0]
        gc = gc_ref[bb]
        lsc_hi, lsc_lo = _split(-_softplus(-gc))
        bc = _dot(tri, lsc_hi) + _dot(tri, lsc_lo)
        lst_hi, lst_lo = _split(-_softplus(-gt))
        gts.append(gt)
        bts.append(_dot(lst_hi, tri_t) + _dot(lst_lo, tri_t))
        xs.append(_split(jnp.where(lane < H, gc, bc)))

    qbs, kbs, v1s, qks, carried, spread = [], [], [], [], [], []
    for bb, h in units:
        qb = qk_ref[bb, :, h * d:(h + 1) * d]
        kb = qk_ref[bb, :, W + h * d:W + (h + 1) * d]
        v = vo_ref[bb, :, h * d:(h + 1) * d]
        qbs.append(qb)
        kbs.append(kb)
        v1s.append(jnp.concatenate([v, ones_cols], axis=1))
        qks.append(_dot_nt(qb, kb))
        carried.append(_dot(qb, s_ref[bb * H + h].astype(BF16)))
        spread.append(_dot(xs[bb][0], pick[h]) + _dot(xs[bb][1], pick[h]))

    ss, m_ts, w_inters = [], [], []
    for u, (bb, h) in enumerate(units):
        b_c = spread[u][:, :d]
        i_r = gts[bb][h:h + 1, :]
        b_r = bts[bb][H + h:H + h + 1, :]
        logd = jnp.where(causal, b_c - b_r + i_r, -jnp.inf)
        inter = b_c + m_ref[u][:1, :]
        m_t = jnp.maximum(inter, jnp.max(logd, axis=-1, keepdims=True))
        ss.append((qks[u] * jnp.exp(logd - m_t)).astype(BF16))
        m_ts.append(m_t)
        w_inters.append(jnp.exp(inter - m_t))

    intras = [_dot(ss[u], v1s[u]) for u in range(len(units))]

    hhs = []
    for u in range(len(units)):
        w2 = jnp.concatenate([w_inters[u], w_inters[u]], axis=1)
        tot = intras[u] + w2 * carried[u]
        hhs.append(tot[:, :d] / jnp.maximum(jnp.abs(tot[:, d:]), jnp.exp(-m_ts[u])))
    sqs = [_dot((hh * hh).astype(BF16), ones_sq) * (1.0 / d) for hh in hhs]
    for u, (bb, h) in enumerate(units):
        og = vo_ref[bb, :, W + h * d:W + (h + 1) * d].astype(F32)
        hn = hhs[u] * lax.rsqrt(sqs[u] + EPS) * hg[:, h * d:(h + 1) * d]
        out_ref[bb, :, h * d:(h + 1) * d] = (_sigmoid(og) * hn).astype(BF16)

    kws, w_olds, m_news = [], [], []
    for u, (bb, h) in enumerate(units):
        b_c, i_c = spread[u][:, :d], spread[u][:, d:]
        i_r = gts[bb][h:h + 1, :]
        b_r = bts[bb][H + h:H + h + 1, :]
        m_prev = m_ref[u][:1, :1]
        b_last = b_r[:, L - 1:L]
        m_new = jnp.maximum(b_last + m_prev, jnp.max(b_last - b_r + i_r, axis=-1, keepdims=True))
        w_olds.append(jnp.exp(b_last + m_prev - m_new))
        m_news.append(m_new)
        kws.append((kbs[u].astype(F32) * jnp.exp(b_last - b_c + i_c - m_new)).T.astype(BF16))
    for u in range(len(units)):
        s_ref[u] = w_olds[u] * s_ref[u] + _dot(kws[u], v1s[u])
        m_ref[u] = jnp.broadcast_to(m_news[u], (8, LANES))


def _mlstm(l, B, S, qka, vo, gc, gt, ml_head_g):
    L, W, H, d = ML_CHUNK, ML_WIDTH, ML_HEADS, ML_DHEAD
    nc = S // L
    nb = 4 if B % 4 == 0 else (2 if B % 2 == 0 else 1)
    out = pl.pallas_call(
        _mlstm_kernel,
        grid=(B // nb, nc),
        in_specs=[
            pl.BlockSpec((nb, L, 2 * W), lambda b, c: (b, c, 0)),
            pl.BlockSpec((nb, L, 2 * W), lambda b, c: (b, c, 0)),
            pl.BlockSpec((nb, L, LANES), lambda b, c: (b, c, 0)),
            pl.BlockSpec((nb, 1, 8, L), lambda b, c: (b, c, 0, 0)),
            pl.BlockSpec((1, 1, W), lambda b, c: (l, 0, 0)),
        ],
        out_specs=pl.BlockSpec((nb, L, W), lambda b, c: (b, c, 0)),
        out_shape=jax.ShapeDtypeStruct((B, S, W), BF16),
        scratch_shapes=[
            pltpu.VMEM((nb * H, d, 2 * d), F32),
            pltpu.VMEM((nb * H, 8, LANES), F32),
        ],
        compiler_params=_params(("arbitrary", "arbitrary")),
        name="mlstm",
    )(qka.reshape(B, S, 2 * W), vo.reshape(B, S, 2 * W), gc.reshape(B, S, LANES), gt.reshape(B, nc, 8, L),
      ml_head_g)
    return out.reshape(B * S, W)


def _sb_kernel(qt_ref, k_ref, vt_ref, g_ref, o_ref, acc_ref):
    P, H, d = SB_SPAN, SB_HEADS, SB_DHEAD
    i = pl.program_id(1)
    row = lax.broadcasted_iota(I32, (P, P), 0)
    col = lax.broadcasted_iota(I32, (P, P), 1)
    strict = row < col
    neg_from = jnp.where(col >= row, -1.0, 0.0).astype(BF16)

    heads = [slice(h * d, (h + 1) * d) for h in range(H)]

    def spans(js, csums, diagonal):
        units = [(n, j, h) for n, j in enumerate(js) for h in range(H)]
        masked = [diagonal and n == 0 for n, _, _ in units]
        ks = [k_ref[pl.ds(pl.multiple_of(j * P, P), P), heads[h]] for _, j, h in units]
        qts = [qt_ref[0, heads[h], :] for _, _, h in units]
        zs = [_dot(kj, qt) for kj, qt in zip(ks, qts)]
        sps = []
        for z, m in zip(zs, masked):
            sp = jnp.maximum(z, 0.0) + jnp.log(1.0 + jnp.exp2(-jnp.abs(z))) * LOG2E
            sps.append(jnp.where(strict, sp, 0.0) if m else sp)
        es = [_dot(jnp.concatenate([kj, neg_from], axis=1), jnp.concatenate([qt, sp.astype(BF16)], axis=0))
              for kj, qt, sp in zip(ks, qts, sps)]
        csums = list(csums) if csums is not None else [None] * H
        parts = [None] * H
        for u, (_, j, h) in enumerate(units):
            csum = jnp.sum(sps[u], axis=0, keepdims=True)
            if masked[u]:
                a = jnp.where(strict, jnp.exp2(es[u]), 0.0)
                csums[h] = csum
            else:
                a = jnp.exp2(es[u] - csums[h])
                csums[h] = csums[h] + csum
            part = _dot(vt_ref[j, heads[h], :], a.astype(BF16))
            parts[h] = part if parts[h] is None else parts[h] + part
        for h in range(H):
            if diagonal:
                acc_ref[heads[h], :] = parts[h]
            else:
                acc_ref[heads[h], :] += parts[h]
        return tuple(csums)

    @pl.when(i == 0)
    def _():
        spans([i], None, True)

    @pl.when(i > 0)
    def _():
        csums = spans([i, i - 1], None, True)
        left = i - 1
        odd = left % 2
        csums = lax.fori_loop(0, odd, lambda jj, c: spans([i - 2], c, False), csums)
        lax.fori_loop(0, left // 2, lambda p, c: spans([i - 2 - odd - 2 * p, i - 3 - odd - 2 * p], c, False), csums)
    g = g_ref[0]
    for h in range(H):
        hs = slice(h * d, (h + 1) * d)
        acc = acc_ref[hs, :]
        on = acc * lax.rsqrt(jnp.mean(acc * acc, axis=0, keepdims=True) + EPS)
        o_ref[:, hs] = (on.T * g[:, hs]).astype(BF16)


def _stickbreak(l, B, S, sqt, sk, svt, sb_head_g):
    P, W = SB_SPAN, SB_WIDTH
    nq = S // P
    return pl.pallas_call(
        _sb_kernel,
        grid=(B, nq),
        in_specs=[
            pl.BlockSpec((1, W, P), lambda b, i: (b * nq + i, 0, 0)),
            pl.BlockSpec((S, W), lambda b, i: (b, 0)),
            pl.BlockSpec((nq, W, P), lambda b, i: (b, 0, 0)),
            pl.BlockSpec((1, 1, W), lambda b, i: (l, 0, 0)),
        ],
        out_specs=pl.BlockSpec((P, W), lambda b, i: (b * nq + i, 0)),
        out_shape=jax.ShapeDtypeStruct((B * S, W), BF16),
        scratch_shapes=[pltpu.VMEM((W, P), F32)],
        compiler_params=_params(("arbitrary", "arbitrary")),
        name="stickbreak",
    )(sqt, sk, svt, sb_head_g)


def _route(lgt, tm):
    G, E = N_GROUPS, EXP_PER_GROUP
    lg = [lgt[i:i + 1, :] for i in range(G + G * E)]
    mg = functools.reduce(jnp.maximum, lg[:G])
    eg = [jnp.exp(v - mg) for v in lg[:G]]
    zg = functools.reduce(jnp.add, eg)
    pg = [v / zg for v in eg]
    p_sel = functools.reduce(jnp.maximum, pg)
    gid = jnp.where(pg[0] == p_sel, 0, jnp.where(pg[1] == p_sel, 1, jnp.where(pg[2] == p_sel, 2, 3)))
    es = [jnp.where(gid == 0, lg[G + e], jnp.where(gid == 1, lg[G + E + e],
          jnp.where(gid == 2, lg[G + 2 * E + e], lg[G + 3 * E + e]))) for e in range(E)]
    me = functools.reduce(jnp.maximum, es)
    ee = [jnp.exp(v - me) for v in es]
    ze = functools.reduce(jnp.add, ee)
    pe = [v / ze for v in ee]

    def first_max(vals):
        top = functools.reduce(jnp.maximum, vals)
        idx = jnp.where(vals[0] == top, 0, jnp.where(vals[1] == top, 1, jnp.where(vals[2] == top, 2, 3)))
        return top, idx

    v1, i1 = first_max(pe)
    v2, i2 = first_max([jnp.where(i1 == e, -1.0, pe[e]) for e in range(E)])
    tsum = v1 + v2
    tw1, tw2 = v1 / tsum, v2 / tsum
    gates = [p_sel * (jnp.where(i1 == e, tw1, 0.0) + jnp.where(i2 == e, tw2, 0.0)) for e in range(E)]

    sub8 = lax.broadcasted_iota(I32, (8, tm), 0)
    onehot = jnp.where(sub8 == gid, 1.0, 0.0)
    r = lax.broadcasted_iota(I32, (SORT_BLOCK, SORT_BLOCK), 0)
    c = lax.broadcasted_iota(I32, (SORT_BLOCK, SORT_BLOCK), 1)
    before = jnp.where(r < c, 1.0, 0.0).astype(BF16)
    lane = lax.broadcasted_iota(I32, (8, LANES), 1)
    counts = jnp.zeros((8, LANES), F32)
    lps = []
    for jj in range(tm // SORT_BLOCK):
        oj = onehot[:, jj * SORT_BLOCK:(jj + 1) * SORT_BLOCK]
        pre = _dot(oj.astype(BF16), before)
        cnt = jnp.sum(oj, axis=-1, keepdims=True)
        pcnt = jnp.floor((cnt + (ROW_TILE_BF16 - 1.0)) * (1.0 / ROW_TILE_BF16)) * ROW_TILE_BF16
        start = jnp.zeros((1, 1), F32)
        lp = jnp.zeros((1, SORT_BLOCK), F32)
        for g in range(G):
            lp = lp + oj[g:g + 1, :] * (pre[g:g + 1, :] + start)
            start = start + pcnt[g:g + 1, :]
        lps.append(lp)
        counts = counts + jnp.where(lane == jj, cnt, 0.0)
    lp = jnp.concatenate(lps, axis=1)
    rows = gates + [gid.astype(F32), lp]
    pack = jnp.zeros((8, tm), F32)
    for kk, v in enumerate(rows):
        pack = pack + jnp.where(sub8 == kk, v, 0.0)
    return pack, counts.astype(I32)


def _post_kernel(has_y, *refs):
    if has_y:
        x_ref, y_ref = refs[:2]
        refs = refs[2:]
        x = x_ref[...] + y_ref[...].astype(F32)
    else:
        x_ref = refs[0]
        refs = refs[1:]
        x = x_ref[...]
    (hml_ref, hsb_ref, wo_ref, gmem_ref, wq_ref, k_ref, v_ref, wmo_ref, gffn_ref, wrh_ref, wrl_ref, br_ref,
     x2_ref, h3_ref, rt_ref, gcol_ref, cnt_ref) = refs
    tm = x.shape[0]
    x1 = x + _dot(hml_ref[...], wo_ref[0, :ML_WIDTH, :]) + _dot(hsb_ref[...], wo_ref[0, ML_WIDTH:, :])
    q = _dot(_rms(x1, gmem_ref[0]).astype(BF16), wq_ref[0]).astype(BF16)
    kk = k_ref[0, 0]
    vv = v_ref[0, 0]
    dh = q.shape[1] // XA_HEADS
    heads = [slice(hd * dh, (hd + 1) * dh) for hd in range(XA_HEADS)]
    scores = [_dot_nt(q[:, sl], kk[:, sl]) * (dh ** -0.5) for sl in heads]
    probs = []
    for s in scores:
        p = jnp.exp(s - jnp.max(s, axis=-1, keepdims=True))
        probs.append((p / jnp.sum(p, axis=-1, keepdims=True)).astype(BF16))
    outs = [_dot(p, vv[:, sl]).astype(BF16) for p, sl in zip(probs, heads)]
    x2 = x1 + _dot(jnp.concatenate(outs, axis=-1), wmo_ref[0])
    x2_ref[...] = x2
    h3_hi, h3_lo = _split(_rms(x2, gffn_ref[0]))
    h3_ref[...] = h3_hi
    wrh = wrh_ref[0]
    lg = _dot(h3_hi, wrh) + _dot(h3_lo, wrh) + _dot(h3_hi, wrl_ref[0]) + br_ref[0]
    pack, counts = _route(lg.T, tm)
    rt_ref[...] = pack
    gcol_ref[...] = jnp.concatenate([pack, jnp.zeros((LANES - 8, tm), F32)], axis=0).T
    cnt_ref[0] = counts


def _post(l, B, S, x2, y, hml, hsb, w_out, ln_mem, w_mq, kx, vx, w_mo, ln_ffn, wr_hi, wr_lo, b_r):
    N, D = x2.shape
    M = kx.shape[2]
    tm = TOKEN_TILE
    tpb = S // tm
    row = lambda i: (i, 0)
    lay = lambda i: (l, 0, 0)
    ins = [x2] + ([y] if y is not None else [])
    in_specs = [pl.BlockSpec((tm, D), row)] * len(ins) + [
        pl.BlockSpec((tm, ML_WIDTH), row),
        pl.BlockSpec((tm, SB_WIDTH), row),
        pl.BlockSpec((1, D, D), lay),
        pl.BlockSpec((1, 1, D), lay),
        pl.BlockSpec((1, D, D), lay),
        pl.BlockSpec((1, 1, M, D), lambda i: (l, i // tpb, 0, 0)),
        pl.BlockSpec((1, 1, M, D), lambda i: (l, i // tpb, 0, 0)),
        pl.BlockSpec((1, D, D), lay),
        pl.BlockSpec((1, 1, D), lay),
        pl.BlockSpec((1, D, LANES), lay),
        pl.BlockSpec((1, D, LANES), lay),
        pl.BlockSpec((1, 1, LANES), lay),
    ]
    return pl.pallas_call(
        functools.partial(_post_kernel, y is not None),
        grid=(N // tm,),
        in_specs=in_specs,
        out_specs=[
            pl.BlockSpec((tm, D), row),
            pl.BlockSpec((tm, D), row),
            pl.BlockSpec((8, tm), lambda i: (0, i)),
            pl.BlockSpec((tm, LANES), row),
            pl.BlockSpec((1, 8, LANES), lambda i: (i, 0, 0)),
        ],
        out_shape=[
            jax.ShapeDtypeStruct((N, D), F32),
            jax.ShapeDtypeStruct((N, D), BF16),
            jax.ShapeDtypeStruct((8, N), F32),
            jax.ShapeDtypeStruct((N, LANES), F32),
            jax.ShapeDtypeStruct((N // tm, 8, LANES), I32),
        ],
        compiler_params=_params(("arbitrary",)),
        name="post",
    )(*ins, hml, hsb, w_out, ln_mem, w_mq, kx, vx, w_mo, ln_ffn, wr_hi, wr_lo, b_r)


def _moe_kernel(T, cnt_ref, h3_ref, rt_ref, gcol_ref, w1_ref, w3_ref, w2_ref, y_ref,
                xs_ref, gs_ref, ys_ref, sl_ref, gl_ref):
    G, CH, CH_TAIL, SBK, WIN, RT = N_GROUPS, MOE_CHUNK, MOE_CHUNK_TAIL, SORT_BLOCK, SORT_WIN, ROW_TILE_BF16
    b, g, hf = pl.program_id(0), pl.program_id(1), pl.program_id(2)
    nsb = T // SBK
    n_half = pl.num_programs(2)
    D = h3_ref.shape[1]

    pc = [[((cnt_ref[(b * nsb + j) * G + gg] + (RT - 1)) // RT) * RT for j in range(nsb)] for gg in range(G)]
    base = [0]
    for gg in range(G):
        base.append(base[-1] + functools.reduce(lambda u, w: u + w, pc[gg]))

    def seg_starts(j):
        out, src = [], 0
        for gg in range(G):
            dst = base[gg]
            for jp in range(j):
                dst = dst + pc[gg][jp]
            out.append((src, dst))
            src = src + pc[gg][j]
        return out

    @pl.when((g == 0) & (hf == 0))
    def _sort():
        total = pl.multiple_of(base[G], RT)
        xs_ref[pl.ds(total, CH), :] = jnp.zeros((CH, D), BF16)
        gs_ref[pl.ds(total, CH), :] = jnp.zeros((CH, LANES), F32)
        riota = lax.broadcasted_iota(I32, (WIN, SBK), 0)
        for j in range(nsb):
            lp = rt_ref[5:6, j * SBK:(j + 1) * SBK].astype(I32)
            perm = jnp.where(riota == lp, 1.0, 0.0).astype(BF16)
            sl_ref[...] = _dot(perm, h3_ref[j * SBK:(j + 1) * SBK, :]).astype(BF16)
            g_hi, g_lo = _split(gcol_ref[j * SBK:(j + 1) * SBK, :])
            gl_ref[...] = _dot(perm, g_hi) + _dot(perm, g_lo)
            for gg, (src, dst) in enumerate(seg_starts(j)):
                def copy_in(c, carry, src=src, dst=dst):
                    s = pl.multiple_of(src + c * RT, RT)
                    t = pl.multiple_of(dst + c * RT, RT)
                    xs_ref[pl.ds(t, RT), :] = sl_ref[pl.ds(s, RT), :]
                    gs_ref[pl.ds(t, RT), :] = gl_ref[pl.ds(s, RT), :]
                    return carry
                lax.fori_loop(0, pc[gg][j] // RT, copy_in, 0)

    rows_g = base[1] - base[0]
    base_g = base[0]
    for gg in range(1, G):
        rows_g = jnp.where(g == gg, base[gg + 1] - base[gg], rows_g)
        base_g = jnp.where(g == gg, base[gg], base_g)

    def run_chunk(r0, rows):
        r0 = pl.multiple_of(r0, RT)
        xc = xs_ref[pl.ds(r0, rows), :]
        gc = gs_ref[pl.ds(r0, rows), :]
        acc = None
        for e in range(EXPERTS_PER_STEP):
            a = _dot(xc, w1_ref[e])
            u = _dot(xc, w3_ref[e])
            gate = gc[:, e:e + 1]
            for k in range(1, EXP_PER_GROUP // EXPERTS_PER_STEP):
                gate = jnp.where(hf == k, gc[:, k * EXPERTS_PER_STEP + e:k * EXPERTS_PER_STEP + e + 1], gate)
            hm = (a * _sigmoid(a)) * u * gate
            part = _dot(hm.astype(BF16), w2_ref[e])
            acc = part if acc is None else acc + part

        @pl.when(hf == 0)
        def _():
            ys_ref[pl.ds(r0, rows), :] = acc

        @pl.when(hf != 0)
        def _():
            ys_ref[pl.ds(r0, rows), :] += acc

    n_big = rows_g // CH
    tail = base_g + n_big * CH
    n_small = (rows_g - n_big * CH + (CH_TAIL - 1)) // CH_TAIL

    def big(c, carry):
        run_chunk(base_g + c * CH, CH)
        return carry

    def small(c, carry):
        run_chunk(tail + c * CH_TAIL, CH_TAIL)
        return carry

    lax.fori_loop(0, n_big, big, 0)
    lax.fori_loop(0, n_small, small, 0)

    @pl.when((g == G - 1) & (hf == n_half - 1))
    def _unsort():
        ciota = lax.broadcasted_iota(I32, (SBK, WIN), 1)
        for j in range(nsb):
            for gg, (src, dst) in enumerate(seg_starts(j)):
                def copy_out(c, carry, src=src, dst=dst):
                    s = pl.multiple_of(src + c * RT, RT)
                    t = pl.multiple_of(dst + c * RT, RT)
                    sl_ref[pl.ds(s, RT), :] = ys_ref[pl.ds(t, RT), :].astype(BF16)
                    return carry
                lax.fori_loop(0, pc[gg][j] // RT, copy_out, 0)
            lp = gcol_ref[j * SBK:(j + 1) * SBK, 5:6].astype(I32)
            perm_t = jnp.where(ciota == lp, 1.0, 0.0).astype(BF16)
            y_ref[j * SBK:(j + 1) * SBK, :] = _dot(perm_t, sl_ref[...]).astype(BF16)


def _moe(l, T, counts, h3, rt, gcol, w1, w3, w2):
    N, D = h3.shape
    G, E, EPS_ = N_GROUPS, EXP_PER_GROUP, EXPERTS_PER_STEP
    F = w1.shape[-1]
    rows = T + (T // SORT_BLOCK) * G * (ROW_TILE_BF16 - 1) + MOE_CHUNK
    rows = -(-rows // ROW_TILE_BF16) * ROW_TILE_BF16
    grid_spec = pltpu.PrefetchScalarGridSpec(
        num_scalar_prefetch=1,
        grid=(N // T, G, E // EPS_),
        in_specs=[
            pl.BlockSpec((T, D), lambda b, g, hf, cnt: (b, 0)),
            pl.BlockSpec((8, T), lambda b, g, hf, cnt: (0, b)),
            pl.BlockSpec((T, LANES), lambda b, g, hf, cnt: (b, 0)),
            pl.BlockSpec((None, None, EPS_, D, F), lambda b, g, hf, cnt: (l, g, hf, 0, 0)),
            pl.BlockSpec((None, None, EPS_, D, F), lambda b, g, hf, cnt: (l, g, hf, 0, 0)),
            pl.BlockSpec((None, None, EPS_, F, D), lambda b, g, hf, cnt: (l, g, hf, 0, 0)),
        ],
        out_specs=pl.BlockSpec((T, D), lambda b, g, hf, cnt: (b, 0)),
        scratch_shapes=[
            pltpu.VMEM((rows, D), BF16),
            pltpu.VMEM((rows, LANES), F32),
            pltpu.VMEM((rows, D), F32),
            pltpu.VMEM((SORT_WIN, D), BF16),
            pltpu.VMEM((SORT_WIN, LANES), F32),
        ],
    )
    return pl.pallas_call(
        functools.partial(_moe_kernel, T),
        grid_spec=grid_spec,
        out_shape=jax.ShapeDtypeStruct((N, D), BF16),
        compiler_params=_params(("arbitrary", "arbitrary", "arbitrary")),
        name="moe",
    )(counts, h3, rt, gcol, w1, w3, w2)


def _final_kernel(x_ref, y_ref, g_ref, o_ref):
    o_ref[...] = _rms(x_ref[...] + y_ref[...].astype(F32), g_ref[...])


def _final(x2, y, ln_final):
    N, D = x2.shape
    tm = TOKEN_TILE
    return pl.pallas_call(
        _final_kernel,
        grid=(N // tm,),
        in_specs=[pl.BlockSpec((tm, D), lambda i: (i, 0)), pl.BlockSpec((tm, D), lambda i: (i, 0)),
                  pl.BlockSpec((1, D), lambda i: (0, 0))],
        out_specs=pl.BlockSpec((tm, D), lambda i: (i, 0)),
        out_shape=jax.ShapeDtypeStruct((N, D), F32),
        compiler_params=_params(("arbitrary",)),
        name="final_norm",
    )(x2, y, ln_final.reshape(1, D))


def kernel(x, mem, ln_mix, w_in, conv_w, conv_b, i_bias, f_bias, ml_head_g, sb_head_g, w_out, ln_mem, ln_memkv,
           w_mq, w_mk, w_mv, w_mo, ln_ffn, w_rg, b_rg, w_re, b_re, w_e1, w_e3, w_e2, ln_final):
    B, S, D = x.shape
    N = B * S
    depth = w_in.shape[0]
    G, E = N_GROUPS, EXP_PER_GROUP
    assert S % TOKEN_TILE == 0 and TOKEN_TILE % ML_CHUNK == 0 and TOKEN_TILE % SORT_BLOCK == 0 and S % SB_SPAN == 0
    T = 2048 if S % 2048 == 0 else S

    c0, c1 = 4 * ML_WIDTH, 4 * ML_WIDTH + 2 * ML_HEADS
    w_ml = w_in[:, :, :c0].astype(BF16)
    w_sqt = jnp.swapaxes(w_in[:, :, c1:c1 + SB_WIDTH], 1, 2).astype(BF16)
    w_sk = w_in[:, :, c1 + SB_WIDTH:c1 + 2 * SB_WIDTH].astype(BF16)
    w_svt = jnp.swapaxes(w_in[:, :, c1 + 2 * SB_WIDTH:], 1, 2).astype(BF16)
    w_gate = w_in[:, :, c0:c1]
    w_g = jnp.pad(w_gate, ((0, 0), (0, 0), (0, LANES - 8))).astype(BF16)
    w_gt = jnp.swapaxes(w_gate, 1, 2).astype(BF16)
    gate_b = jnp.concatenate([i_bias, f_bias], axis=-1)
    b_g = jnp.pad(gate_b, ((0, 0), (0, LANES - 8))).reshape(depth, 1, LANES)
    b_gt = gate_b.reshape(depth, 8, 1)
    w_r = jnp.pad(jnp.concatenate([w_rg, w_re], axis=-1), ((0, 0), (0, 0), (0, LANES - G - G * E)))
    wr_hi = w_r.astype(BF16)
    wr_lo = (w_r - wr_hi.astype(F32)).astype(BF16)
    b_r = jnp.pad(jnp.concatenate([b_rg, b_re], axis=-1), ((0, 0), (0, LANES - G - G * E))).reshape(depth, 1, LANES)
    w_out_b, w_mq_b, w_mo_b = w_out.astype(BF16), w_mq.astype(BF16), w_mo.astype(BF16)
    w1, w3, w2 = w_e1.astype(BF16), w_e3.astype(BF16), w_e2.astype(BF16)
    r3 = lambda a: a.reshape(depth, 1, a.shape[-1])

    kx, vx = _memkv(mem, ln_memkv, w_mk.astype(BF16), w_mv.astype(BF16))

    x2, y = x.reshape(N, D), None
    nsub = TOKEN_TILE // SORT_BLOCK
    for l in range(depth):
        qka, vo, sqt, sk, svt, gc, gt = _proj(l, S, x2, y, r3(ln_mix), w_ml, w_sqt, w_sk, w_svt, w_g, w_gt, b_g, b_gt,
                                         conv_w, r3(conv_b))
        hml = _mlstm(l, B, S, qka, vo, gc, gt, r3(ml_head_g))
        hsb = _stickbreak(l, B, S, sqt, sk, svt, r3(sb_head_g))
        x2, h3, rt, gcol, cnt = _post(l, B, S, x2, y, hml, hsb, w_out_b, r3(ln_mem), w_mq_b, kx, vx, w_mo_b,
                                      r3(ln_ffn), wr_hi, wr_lo, b_r)
        counts = jnp.swapaxes(cnt[:, :G, :nsub], 1, 2).reshape(-1)
        y = _moe(l, T, counts, h3, rt, gcol, w1, w3, w2)
    return _final(x2, y, ln_final).reshape(B, S, D)
```

```python
import functools

import jax
import jax.numpy as jnp
from jax import lax
from jax.experimental import pallas as pl
from jax.experimental.pallas import tpu as pltpu

F32 = jnp.float32
BF16 = jnp.bfloat16
I32 = jnp.int32

ML_HEADS = 4
ML_DHEAD = 128
SB_HEADS = 4
SB_DHEAD = 128
ML_WIDTH = ML_HEADS * ML_DHEAD
SB_WIDTH = SB_HEADS * SB_DHEAD
CONV_W = 4
ML_CHUNK = 128
SB_SPAN = 256
XA_HEADS = 4
N_GROUPS = 4
EXP_PER_GROUP = 4
EPS = 1e-6
LOG2E = 1.4426950408889634

LANES = 128
ROW_TILE_BF16 = 16
SORT_BLOCK = 256
SORT_WIN = 384
MOE_CHUNK = 256
MOE_CHUNK_TAIL = 128
EXPERTS_PER_STEP = 2
TOKEN_TILE = 1024
VMEM_LIMIT = 56 * 1024 * 1024


def _dot(a, b):
    return jnp.dot(a, b, preferred_element_type=F32)


def _dot_nt(a, b):
    return lax.dot_general(a, b, (((1,), (1,)), ((), ())), preferred_element_type=F32)


def _split(x):
    hi = x.astype(BF16)
    lo = (x - hi.astype(F32)).astype(BF16)
    return hi, lo


def _rms(x, g):
    return x * lax.rsqrt(jnp.mean(x * x, axis=-1, keepdims=True) + EPS) * g


def _sigmoid(x):
    return 1.0 / (1.0 + jnp.exp(-x))


def _softplus(x):
    return jnp.maximum(x, 0.0) + jnp.log1p(jnp.exp(-jnp.abs(x)))


def _params(sem):
    return pltpu.CompilerParams(dimension_semantics=sem, vmem_limit_bytes=VMEM_LIMIT)


def _memkv_kernel(mem_ref, g_ref, wk_ref, wv_ref, k_ref, v_ref):
    h = _rms(mem_ref[0], g_ref[0]).astype(BF16)
    k_ref[0, 0] = _dot(h, wk_ref[0]).astype(BF16)
    v_ref[0, 0] = _dot(h, wv_ref[0]).astype(BF16)


def _memkv(mem, ln_memkv, w_mk, w_mv):
    B, M, D = mem.shape
    depth = w_mk.shape[0]
    return pl.pallas_call(
        _memkv_kernel,
        grid=(depth, B),
        in_specs=[
            pl.BlockSpec((1, M, D), lambda l, b: (b, 0, 0)),
            pl.BlockSpec((1, 1, D), lambda l, b: (l, 0, 0)),
            pl.BlockSpec((1, D, D), lambda l, b: (l, 0, 0)),
            pl.BlockSpec((1, D, D), lambda l, b: (l, 0, 0)),
        ],
        out_specs=[
            pl.BlockSpec((1, 1, M, D), lambda l, b: (l, b, 0, 0)),
            pl.BlockSpec((1, 1, M, D), lambda l, b: (l, b, 0, 0)),
        ],
        out_shape=[jax.ShapeDtypeStruct((depth, B, M, D), BF16)] * 2,
        compiler_params=_params(("arbitrary", "arbitrary")),
        name="memkv",
    )(mem, ln_memkv.reshape(depth, 1, D), w_mk, w_mv)


def _proj_kernel(has_y, tiles_per_seq, *refs):
    if has_y:
        x_ref, y_ref = refs[:2]
        refs = refs[2:]
        x = x_ref[...] + y_ref[...].astype(F32)
    else:
        x_ref = refs[0]
        refs = refs[1:]
        x = x_ref[...]
    (g_ref, wml_ref, wsqt_ref, wsk_ref, wsvt_ref, wg_ref, bg_ref, cw_ref, cb_ref,
     qka_ref, vo_ref, sqt_ref, sk_ref, svt_ref, gc_ref, gt_ref, qk_ref) = refs
    tm, W, L, P = x.shape[0], ML_WIDTH, ML_CHUNK, SB_SPAN

    @pl.when(pl.program_id(0) % tiles_per_seq == 0)
    def _():
        qk_ref[:8, :] = jnp.zeros((8, 2 * W), F32)

    h = _rms(x, g_ref[0]).astype(BF16)
    cw = cw_ref[0]
    cb = cb_ref[0]
    cblk = 2 * LANES

    def conv_block(c):
        rows = L
        for r in range(0, tm, rows):
            acc = cb[:, c:c + cblk]
            for tap in range(CONV_W):
                lo = 8 - (CONV_W - 1) + tap + r
                acc = acc + qk_ref[lo:lo + rows, c:c + cblk] * cw[tap:tap + 1, c:c + cblk]
            qa = acc * _sigmoid(acc)
            if c >= W:
                qa = qa * (ML_DHEAD ** -0.5)
            qka_ref[r:r + rows, c:c + cblk] = qa.astype(BF16)
        qk_ref[:8, c:c + cblk] = qk_ref[tm:, c:c + cblk]

    def store_v():
        vo_ref[:, :W] = _dot(h, wml_ref[0, :, 2 * W:3 * W]).astype(BF16)

    def store_o():
        vo_ref[:, W:] = _dot(h, wml_ref[0, :, 3 * W:]).astype(BF16)

    def store_sk():
        sk_ref[...] = _dot(h, wsk_ref[0]).astype(BF16)

    def store_sqt():
        sqt = (_dot_nt(wsqt_ref[0], h) * (SB_DHEAD ** -0.5 * LOG2E)).astype(BF16)
        for c in range(tm // P):
            sqt_ref[c] = sqt[:, c * P:(c + 1) * P]

    def store_svt():
        svt = _dot_nt(wsvt_ref[0], h).astype(BF16)
        for c in range(tm // P):
            svt_ref[c] = svt[:, c * P:(c + 1) * P]

    others = [store_v, store_o, store_sk, store_sqt, store_svt]
    blocks = list(range(0, 2 * W, cblk))
    for n, c in enumerate(blocks):
        qk_ref[8:, c:c + cblk] = _dot(h, wml_ref[0, :, c:c + cblk])
        if n > 0:
            conv_block(blocks[n - 1])
        if others:
            others.pop(0)()
    conv_block(blocks[-1])
    for rest in others:
        rest()
    gc = _dot(h, wg_ref[0]) + bg_ref[0]
    gc_ref[...] = gc
    gt = gc.T
    for c in range(tm // L):
        gt_ref[c] = gt[:8, c * L:(c + 1) * L]


def _proj(l, S, x2, y, ln_mix, w_ml, w_sqt, w_sk, w_svt, w_g, b_g, conv_w, conv_b):
    N, D = x2.shape
    tm, W, L, P, SW = TOKEN_TILE, ML_WIDTH, ML_CHUNK, SB_SPAN, SB_WIDTH
    row = lambda i: (i, 0)
    lay = lambda i: (l, 0, 0)
    ins = [x2] + ([y] if y is not None else [])
    in_specs = [pl.BlockSpec((tm, D), row)] * len(ins) + [
        pl.BlockSpec((1, 1, D), lay),
        pl.BlockSpec((1, D, 4 * W), lay),
        pl.BlockSpec((1, SW, D), lay),
        pl.BlockSpec((1, D, SW), lay),
        pl.BlockSpec((1, SW, D), lay),
        pl.BlockSpec((1, D, LANES), lay),
        pl.BlockSpec((1, 1, LANES), lay),
        pl.BlockSpec((1, CONV_W, 2 * W), lay),
        pl.BlockSpec((1, 1, 2 * W), lay),
    ]
    return pl.pallas_call(
        functools.partial(_proj_kernel, y is not None, S // tm),
        grid=(N // tm,),
        in_specs=in_specs,
        out_specs=[
            pl.BlockSpec((tm, 2 * W), row),
            pl.BlockSpec((tm, 2 * W), row),
            pl.BlockSpec((tm // P, SW, P), lambda i: (i, 0, 0)),
            pl.BlockSpec((tm, SW), row),
            pl.BlockSpec((tm // P, SW, P), lambda i: (i, 0, 0)),
            pl.BlockSpec((tm, LANES), row),
            pl.BlockSpec((tm // L, 8, L), lambda i: (i, 0, 0)),
        ],
        out_shape=[
            jax.ShapeDtypeStruct((N, 2 * W), BF16),
            jax.ShapeDtypeStruct((N, 2 * W), BF16),
            jax.ShapeDtypeStruct((N // P, SW, P), BF16),
            jax.ShapeDtypeStruct((N, SW), BF16),
            jax.ShapeDtypeStruct((N // P, SW, P), BF16),
            jax.ShapeDtypeStruct((N, LANES), F32),
            jax.ShapeDtypeStruct((N // L, 8, L), F32),
        ],
        scratch_shapes=[pltpu.VMEM((8 + tm, 2 * W), F32)],
        compiler_params=_params(("arbitrary",)),
        name="proj",
    )(*ins, ln_mix, w_ml, w_sqt, w_sk, w_svt, w_g, b_g, conv_w, conv_b)


def _mlstm_kernel(qk_ref, vo_ref, gc_ref, gt_ref, hg_ref, out_ref, s_ref, m_ref):
    L, d, W, H = ML_CHUNK, ML_DHEAD, ML_WIDTH, ML_HEADS
    nb = qk_ref.shape[0]

    @pl.when(pl.program_id(1) == 0)
    def _():
        s_ref[...] = jnp.zeros_like(s_ref)
        m_ref[...] = jnp.zeros_like(m_ref)

    row = lax.broadcasted_iota(I32, (L, L), 0)
    col = lax.broadcasted_iota(I32, (L, L), 1)
    causal = col <= row
    tri = jnp.where(causal, 1.0, 0.0).astype(BF16)
    tri_t = jnp.where(row <= col, 1.0, 0.0).astype(BF16)
    lane = lax.broadcasted_iota(I32, (L, LANES), 1)
    ones_cols = jnp.ones((L, d), BF16)
    ones_sq = jnp.ones((d, d), BF16)
    r2 = lax.broadcasted_iota(I32, (LANES, 2 * d), 0)
    c2 = lax.broadcasted_iota(I32, (LANES, 2 * d), 1)
    pick = [jnp.where(((c2 < d) & (r2 == H + h)) | ((c2 >= d) & (r2 == h)), 1.0, 0.0).astype(BF16)
            for h in range(H)]
    hg = hg_ref[0]
    units = [(bb, h) for bb in range(nb) for h in range(H)]

    gts, bts, xs = [], [], []
    for bb in range(nb):
        gt = gt_ref[bb, 0]
        gc = gc_ref[bb]
        lsc_hi, lsc_lo = _split(-_softplus(-gc))
        bc = _dot(tri, lsc_hi) + _dot(tri, lsc_lo)
        lst_hi, lst_lo = _split(-_softplus(-gt))
        gts.append(gt)
        bts.append(_dot(lst_hi, tri_t) + _dot(lst_lo, tri_t))
        xs.append(_split(jnp.where(lane < H, gc, bc)))

    qbs, kbs, v1s, qks, carried, spread = [], [], [], [], [], []
    for bb, h in units:
        qb = qk_ref[bb, :, h * d:(h + 1) * d]
        kb = qk_ref[bb, :, W + h * d:W + (h + 1) * d]
        v = vo_ref[bb, :, h * d:(h + 1) * d]
        qbs.append(qb)
        kbs.append(kb)
        v1s.append(jnp.concatenate([v, ones_cols], axis=1))
        qks.append(_dot_nt(qb, kb))
        carried.append(_dot(qb, s_ref[bb * H + h].astype(BF16)))
        spread.append(_dot(xs[bb][0], pick[h]) + _dot(xs[bb][1], pick[h]))

    ss, m_ts, w_inters = [], [], []
    for u, (bb, h) in enumerate(units):
        b_c = spread[u][:, :d]
        i_r = gts[bb][h:h + 1, :]
        b_r = bts[bb][H + h:H + h + 1, :]
        logd = jnp.where(causal, b_c - b_r + i_r, -jnp.inf)
        inter = b_c + m_ref[u][:1, :]
        m_t = jnp.maximum(inter, jnp.max(logd, axis=-1, keepdims=True))
        ss.append((qks[u] * jnp.exp(logd - m_t)).astype(BF16))
        m_ts.append(m_t)
        w_inters.append(jnp.exp(inter - m_t))

    intras = [_dot(ss[u], v1s[u]) for u in range(len(units))]

    hhs = []
    for u in range(len(units)):
        w2 = jnp.concatenate([w_inters[u], w_inters[u]], axis=1)
        tot = intras[u] + w2 * carried[u]
        hhs.append(tot[:, :d] / jnp.maximum(jnp.abs(tot[:, d:]), jnp.exp(-m_ts[u])))
    sqs = [_dot((hh * hh).astype(BF16), ones_sq) * (1.0 / d) for hh in hhs]
    for u, (bb, h) in enumerate(units):
        og = vo_ref[bb, :, W + h * d:W + (h + 1) * d].astype(F32)
        hn = hhs[u] * lax.rsqrt(sqs[u] + EPS) * hg[:, h * d:(h + 1) * d]
        out_ref[bb, :, h * d:(h + 1) * d] = (_sigmoid(og) * hn).astype(BF16)

    kws, w_olds, m_news = [], [], []
    for u, (bb, h) in enumerate(units):
        b_c, i_c = spread[u][:, :d], spread[u][:, d:]
        i_r = gts[bb][h:h + 1, :]
        b_r = bts[bb][H + h:H + h + 1, :]
        m_prev = m_ref[u][:1, :1]
        b_last = b_r[:, L - 1:L]
        m_new = jnp.maximum(b_last + m_prev, jnp.max(b_last - b_r + i_r, axis=-1, keepdims=True))
        w_olds.append(jnp.exp(b_last + m_prev - m_new))
        m_news.append(m_new)
        kws.append((kbs[u].astype(F32) * jnp.exp(b_last - b_c + i_c - m_new)).T.astype(BF16))
    for u in range(len(units)):
        s_ref[u] = w_olds[u] * s_ref[u] + _dot(kws[u], v1s[u])
        m_ref[u] = jnp.broadcast_to(m_news[u], (8, LANES))


def _mlstm(l, B, S, qka, vo, gc, gt, ml_head_g):
    L, W, H, d = ML_CHUNK, ML_WIDTH, ML_HEADS, ML_DHEAD
    nc = S // L
    nb = 4 if B % 4 == 0 else (2 if B % 2 == 0 else 1)
    out = pl.pallas_call(
        _mlstm_kernel,
        grid=(B // nb, nc),
        in_specs=[
            pl.BlockSpec((nb, L, 2 * W), lambda b, c: (b, c, 0)),
            pl.BlockSpec((nb, L, 2 * W), lambda b, c: (b, c, 0)),
            pl.BlockSpec((nb, L, LANES), lambda b, c: (b, c, 0)),
            pl.BlockSpec((nb, 1, 8, L), lambda b, c: (b, c, 0, 0)),
            pl.BlockSpec((1, 1, W), lambda b, c: (l, 0, 0)),
        ],
        out_specs=pl.BlockSpec((nb, L, W), lambda b, c: (b, c, 0)),
        out_shape=jax.ShapeDtypeStruct((B, S, W), BF16),
        scratch_shapes=[
            pltpu.VMEM((nb * H, d, 2 * d), F32),
            pltpu.VMEM((nb * H, 8, LANES), F32),
        ],
        compiler_params=_params(("arbitrary", "arbitrary")),
        name="mlstm",
    )(qka.reshape(B, S, 2 * W), vo.reshape(B, S, 2 * W), gc.reshape(B, S, LANES), gt.reshape(B, nc, 8, L),
      ml_head_g)
    return out.reshape(B * S, W)


def _sb_kernel(qt_ref, k_ref, vt_ref, g_ref, o_ref, acc_ref):
    P, H, d = SB_SPAN, SB_HEADS, SB_DHEAD
    i = pl.program_id(1)
    row = lax.broadcasted_iota(I32, (P, P), 0)
    col = lax.broadcasted_iota(I32, (P, P), 1)
    strict = row < col
    neg_from = jnp.where(col >= row, -1.0, 0.0).astype(BF16)

    heads = [slice(h * d, (h + 1) * d) for h in range(H)]

    def spans(js, csums, diagonal):
        units = [(n, j, h) for n, j in enumerate(js) for h in range(H)]
        masked = [diagonal and n == 0 for n, _, _ in units]
        ks = [k_ref[pl.ds(pl.multiple_of(j * P, P), P), heads[h]] for _, j, h in units]
        qts = [qt_ref[0, heads[h], :] for _, _, h in units]
        zs = [_dot(kj, qt) for kj, qt in zip(ks, qts)]
        sps = []
        for z, m in zip(zs, masked):
            sp = jnp.maximum(z, 0.0) + jnp.log(1.0 + jnp.exp2(-jnp.abs(z))) * LOG2E
            sps.append(jnp.where(strict, sp, 0.0) if m else sp)
        es = [z + _dot(neg_from, sp.astype(BF16)) for z, sp in zip(zs, sps)]
        csums = list(csums) if csums is not None else [None] * H
        parts = [None] * H
        for u, (_, j, h) in enumerate(units):
            csum = jnp.sum(sps[u], axis=0, keepdims=True)
            if masked[u]:
                a = jnp.where(strict, jnp.exp2(es[u]), 0.0)
                csums[h] = csum
            else:
                a = jnp.exp2(es[u] - csums[h])
                csums[h] = csums[h] + csum
            part = _dot(vt_ref[j, heads[h], :], a.astype(BF16))
            parts[h] = part if parts[h] is None else parts[h] + part
        for h in range(H):
            if diagonal:
                acc_ref[heads[h], :] = parts[h]
            else:
                acc_ref[heads[h], :] += parts[h]
        return tuple(csums)

    @pl.when(i == 0)
    def _():
        spans([i], None, True)

    @pl.when(i > 0)
    def _():
        csums = spans([i, i - 1], None, True)
        left = i - 1
        odd = left % 2
        csums = lax.fori_loop(0, odd, lambda jj, c: spans([i - 2], c, False), csums)
        lax.fori_loop(0, left // 2, lambda p, c: spans([i - 2 - odd - 2 * p, i - 3 - odd - 2 * p], c, False), csums)
    g = g_ref[0]
    for h in range(H):
        hs = slice(h * d, (h + 1) * d)
        acc = acc_ref[hs, :]
        on = acc * lax.rsqrt(jnp.mean(acc * acc, axis=0, keepdims=True) + EPS)
        o_ref[:, hs] = (on.T * g[:, hs]).astype(BF16)


def _stickbreak(l, B, S, sqt, sk, svt, sb_head_g):
    P, W = SB_SPAN, SB_WIDTH
    nq = S // P
    return pl.pallas_call(
        _sb_kernel,
        grid=(B, nq),
        in_specs=[
            pl.BlockSpec((1, W, P), lambda b, i: (b * nq + i, 0, 0)),
            pl.BlockSpec((S, W), lambda b, i: (b, 0)),
            pl.BlockSpec((nq, W, P), lambda b, i: (b, 0, 0)),
            pl.BlockSpec((1, 1, W), lambda b, i: (l, 0, 0)),
        ],
        out_specs=pl.BlockSpec((P, W), lambda b, i: (b * nq + i, 0)),
        out_shape=jax.ShapeDtypeStruct((B * S, W), BF16),
        scratch_shapes=[pltpu.VMEM((W, P), F32)],
        compiler_params=_params(("arbitrary", "arbitrary")),
        name="stickbreak",
    )(sqt, sk, svt, sb_head_g)


def _route(lgt, tm):
    G, E = N_GROUPS, EXP_PER_GROUP
    lg = [lgt[i:i + 1, :] for i in range(G + G * E)]
    mg = functools.reduce(jnp.maximum, lg[:G])
    eg = [jnp.exp(v - mg) for v in lg[:G]]
    zg = functools.reduce(jnp.add, eg)
    pg = [v / zg for v in eg]
    p_sel = functools.reduce(jnp.maximum, pg)
    gid = jnp.where(pg[0] == p_sel, 0, jnp.where(pg[1] == p_sel, 1, jnp.where(pg[2] == p_sel, 2, 3)))
    es = [jnp.where(gid == 0, lg[G + e], jnp.where(gid == 1, lg[G + E + e],
          jnp.where(gid == 2, lg[G + 2 * E + e], lg[G + 3 * E + e]))) for e in range(E)]
    me = functools.reduce(jnp.maximum, es)
    ee = [jnp.exp(v - me) for v in es]
    ze = functools.reduce(jnp.add, ee)
    pe = [v / ze for v in ee]

    def first_max(vals):
        top = functools.reduce(jnp.maximum, vals)
        idx = jnp.where(vals[0] == top, 0, jnp.where(vals[1] == top, 1, jnp.where(vals[2] == top, 2, 3)))
        return top, idx

    v1, i1 = first_max(pe)
    v2, i2 = first_max([jnp.where(i1 == e, -1.0, pe[e]) for e in range(E)])
    tsum = v1 + v2
    tw1, tw2 = v1 / tsum, v2 / tsum
    gates = [p_sel * (jnp.where(i1 == e, tw1, 0.0) + jnp.where(i2 == e, tw2, 0.0)) for e in range(E)]

    sub8 = lax.broadcasted_iota(I32, (8, tm), 0)
    onehot = jnp.where(sub8 == gid, 1.0, 0.0)
    r = lax.broadcasted_iota(I32, (SORT_BLOCK, SORT_BLOCK), 0)
    c = lax.broadcasted_iota(I32, (SORT_BLOCK, SORT_BLOCK), 1)
    before = jnp.where(r < c, 1.0, 0.0).astype(BF16)
    lane = lax.broadcasted_iota(I32, (8, LANES), 1)
    counts = jnp.zeros((8, LANES), F32)
    lps = []
    for jj in range(tm // SORT_BLOCK):
        oj = onehot[:, jj * SORT_BLOCK:(jj + 1) * SORT_BLOCK]
        pre = _dot(oj.astype(BF16), before)
        cnt = jnp.sum(oj, axis=-1, keepdims=True)
        pcnt = jnp.floor((cnt + (ROW_TILE_BF16 - 1.0)) * (1.0 / ROW_TILE_BF16)) * ROW_TILE_BF16
        start = jnp.zeros((1, 1), F32)
        lp = jnp.zeros((1, SORT_BLOCK), F32)
        for g in range(G):
            lp = lp + oj[g:g + 1, :] * (pre[g:g + 1, :] + start)
            start = start + pcnt[g:g + 1, :]
        lps.append(lp)
        counts = counts + jnp.where(lane == jj, cnt, 0.0)
    lp = jnp.concatenate(lps, axis=1)
    rows = gates + [gid.astype(F32), lp]
    pack = jnp.zeros((8, tm), F32)
    for kk, v in enumerate(rows):
        pack = pack + jnp.where(sub8 == kk, v, 0.0)
    return pack, counts.astype(I32)


def _post_kernel(has_y, *refs):
    if has_y:
        x_ref, y_ref = refs[:2]
        refs = refs[2:]
        x = x_ref[...] + y_ref[...].astype(F32)
    else:
        x_ref = refs[0]
        refs = refs[1:]
        x = x_ref[...]
    (hml_ref, hsb_ref, wo_ref, gmem_ref, wq_ref, k_ref, v_ref, wmo_ref, gffn_ref, wrh_ref, wrl_ref, br_ref,
     x2_ref, h3_ref, rt_ref, gcol_ref, cnt_ref) = refs
    tm = x.shape[0]
    x1 = x + _dot(hml_ref[...], wo_ref[0, :ML_WIDTH, :]) + _dot(hsb_ref[...], wo_ref[0, ML_WIDTH:, :])
    q = _dot(_rms(x1, gmem_ref[0]).astype(BF16), wq_ref[0]).astype(BF16)
    kk = k_ref[0, 0]
    vv = v_ref[0, 0]
    dh = q.shape[1] // XA_HEADS
    heads = [slice(hd * dh, (hd + 1) * dh) for hd in range(XA_HEADS)]
    scores = [_dot_nt(q[:, sl], kk[:, sl]) * (dh ** -0.5) for sl in heads]
    probs = []
    for s in scores:
        p = jnp.exp(s - jnp.max(s, axis=-1, keepdims=True))
        probs.append((p / jnp.sum(p, axis=-1, keepdims=True)).astype(BF16))
    outs = [_dot(p, vv[:, sl]).astype(BF16) for p, sl in zip(probs, heads)]
    x2 = x1 + _dot(jnp.concatenate(outs, axis=-1), wmo_ref[0])
    x2_ref[...] = x2
    h3_hi, h3_lo = _split(_rms(x2, gffn_ref[0]))
    h3_ref[...] = h3_hi
    wrh = wrh_ref[0]
    both = _dot(h3_hi, jnp.concatenate([wrh, wrl_ref[0]], axis=1))
    lg = both[:, :LANES] + both[:, LANES:] + _dot(h3_lo, wrh) + br_ref[0]
    pack, counts = _route(lg.T, tm)
    rt_ref[...] = pack
    gcol_ref[...] = jnp.concatenate([pack, jnp.zeros((LANES - 8, tm), F32)], axis=0).T
    cnt_ref[0] = counts


def _post(l, B, S, x2, y, hml, hsb, w_out, ln_mem, w_mq, kx, vx, w_mo, ln_ffn, wr_hi, wr_lo, b_r):
    N, D = x2.shape
    M = kx.shape[2]
    tm = TOKEN_TILE
    tpb = S // tm
    row = lambda i: (i, 0)
    lay = lambda i: (l, 0, 0)
    ins = [x2] + ([y] if y is not None else [])
    in_specs = [pl.BlockSpec((tm, D), row)] * len(ins) + [
        pl.BlockSpec((tm, ML_WIDTH), row),
        pl.BlockSpec((tm, SB_WIDTH), row),
        pl.BlockSpec((1, D, D), lay),
        pl.BlockSpec((1, 1, D), lay),
        pl.BlockSpec((1, D, D), lay),
        pl.BlockSpec((1, 1, M, D), lambda i: (l, i // tpb, 0, 0)),
        pl.BlockSpec((1, 1, M, D), lambda i: (l, i // tpb, 0, 0)),
        pl.BlockSpec((1, D, D), lay),
        pl.BlockSpec((1, 1, D), lay),
        pl.BlockSpec((1, D, LANES), lay),
        pl.BlockSpec((1, D, LANES), lay),
        pl.BlockSpec((1, 1, LANES), lay),
    ]
    return pl.pallas_call(
        functools.partial(_post_kernel, y is not None),
        grid=(N // tm,),
        in_specs=in_specs,
        out_specs=[
            pl.BlockSpec((tm, D), row),
            pl.BlockSpec((tm, D), row),
            pl.BlockSpec((8, tm), lambda i: (0, i)),
            pl.BlockSpec((tm, LANES), row),
            pl.BlockSpec((1, 8, LANES), lambda i: (i, 0, 0)),
        ],
        out_shape=[
            jax.ShapeDtypeStruct((N, D), F32),
            jax.ShapeDtypeStruct((N, D), BF16),
            jax.ShapeDtypeStruct((8, N), F32),
            jax.ShapeDtypeStruct((N, LANES), F32),
            jax.ShapeDtypeStruct((N // tm, 8, LANES), I32),
        ],
        compiler_params=_params(("arbitrary",)),
        name="post",
    )(*ins, hml, hsb, w_out, ln_mem, w_mq, kx, vx, w_mo, ln_ffn, wr_hi, wr_lo, b_r)


def _moe_kernel(T, cnt_ref, h3_ref, rt_ref, gcol_ref, w1_ref, w3_ref, w2_ref, y_ref,
                xs_ref, gs_ref, ys_ref, sl_ref, gl_ref):
    G, CH, CH_TAIL, SBK, WIN, RT = N_GROUPS, MOE_CHUNK, MOE_CHUNK_TAIL, SORT_BLOCK, SORT_WIN, ROW_TILE_BF16
    b, g, hf = pl.program_id(0), pl.program_id(1), pl.program_id(2)
    nsb = T // SBK
    n_half = pl.num_programs(2)
    D = h3_ref.shape[1]

    pc = [[((cnt_ref[(b * nsb + j) * G + gg] + (RT - 1)) // RT) * RT for j in range(nsb)] for gg in range(G)]
    base = [0]
    for gg in range(G):
        base.append(base[-1] + functools.reduce(lambda u, w: u + w, pc[gg]))

    def seg_starts(j):
        out, src = [], 0
        for gg in range(G):
            dst = base[gg]
            for jp in range(j):
                dst = dst + pc[gg][jp]
            out.append((src, dst))
            src = src + pc[gg][j]
        return out

    @pl.when((g == 0) & (hf == 0))
    def _sort():
        total = pl.multiple_of(base[G], RT)
        xs_ref[pl.ds(total, CH), :] = jnp.zeros((CH, D), BF16)
        gs_ref[pl.ds(total, CH), :] = jnp.zeros((CH, LANES), F32)
        riota = lax.broadcasted_iota(I32, (WIN, SBK), 0)
        for j in range(nsb):
            lp = rt_ref[5:6, j * SBK:(j + 1) * SBK].astype(I32)
            perm = jnp.where(riota == lp, 1.0, 0.0).astype(BF16)
            sl_ref[...] = _dot(perm, h3_ref[j * SBK:(j + 1) * SBK, :]).astype(BF16)
            g_hi, g_lo = _split(gcol_ref[j * SBK:(j + 1) * SBK, :])
            both = _dot(perm, jnp.concatenate([g_hi, g_lo], axis=1))
            gl_ref[...] = both[:, :LANES] + both[:, LANES:]
            for gg, (src, dst) in enumerate(seg_starts(j)):
                def copy_in(c, carry, src=src, dst=dst):
                    s = pl.multiple_of(src + c * RT, RT)
                    t = pl.multiple_of(dst + c * RT, RT)
                    xs_ref[pl.ds(t, RT), :] = sl_ref[pl.ds(s, RT), :]
                    gs_ref[pl.ds(t, RT), :] = gl_ref[pl.ds(s, RT), :]
                    return carry
                lax.fori_loop(0, pc[gg][j] // RT, copy_in, 0)

    rows_g = base[1] - base[0]
    base_g = base[0]
    for gg in range(1, G):
        rows_g = jnp.where(g == gg, base[gg + 1] - base[gg], rows_g)
        base_g = jnp.where(g == gg, base[gg], base_g)

    def run_chunk(r0, rows):
        r0 = pl.multiple_of(r0, RT)
        xc = xs_ref[pl.ds(r0, rows), :]
        gc = gs_ref[pl.ds(r0, rows), :]
        acc = None
        for e in range(EXPERTS_PER_STEP):
            a = _dot(xc, w1_ref[e])
            u = _dot(xc, w3_ref[e])
            gate = gc[:, e:e + 1]
            for k in range(1, EXP_PER_GROUP // EXPERTS_PER_STEP):
                gate = jnp.where(hf == k, gc[:, k * EXPERTS_PER_STEP + e:k * EXPERTS_PER_STEP + e + 1], gate)
            hm = (a * _sigmoid(a)) * u * gate
            part = _dot(hm.astype(BF16), w2_ref[e])
            acc = part if acc is None else acc + part

        @pl.when(hf == 0)
        def _():
            ys_ref[pl.ds(r0, rows), :] = acc

        @pl.when(hf != 0)
        def _():
            ys_ref[pl.ds(r0, rows), :] += acc

    n_big = rows_g // CH
    tail = base_g + n_big * CH
    n_small = (rows_g - n_big * CH + (CH_TAIL - 1)) // CH_TAIL

    def big(c, carry):
        run_chunk(base_g + c * CH, CH)
        return carry

    def small(c, carry):
        run_chunk(tail + c * CH_TAIL, CH_TAIL)
        return carry

    lax.fori_loop(0, n_big, big, 0)
    lax.fori_loop(0, n_small, small, 0)

    @pl.when((g == G - 1) & (hf == n_half - 1))
    def _unsort():
        ciota = lax.broadcasted_iota(I32, (SBK, WIN), 1)
        for j in range(nsb):
            for gg, (src, dst) in enumerate(seg_starts(j)):
                def copy_out(c, carry, src=src, dst=dst):
                    s = pl.multiple_of(src + c * RT, RT)
                    t = pl.multiple_of(dst + c * RT, RT)
                    sl_ref[pl.ds(s, RT), :] = ys_ref[pl.ds(t, RT), :].astype(BF16)
                    return carry
                lax.fori_loop(0, pc[gg][j] // RT, copy_out, 0)
            lp = gcol_ref[j * SBK:(j + 1) * SBK, 5:6].astype(I32)
            perm_t = jnp.where(ciota == lp, 1.0, 0.0).astype(BF16)
            y_ref[j * SBK:(j + 1) * SBK, :] = _dot(perm_t, sl_ref[...]).astype(BF16)


def _moe(l, T, counts, h3, rt, gcol, w1, w3, w2):
    N, D = h3.shape
    G, E, EPS_ = N_GROUPS, EXP_PER_GROUP, EXPERTS_PER_STEP
    F = w1.shape[-1]
    rows = T + (T // SORT_BLOCK) * G * (ROW_TILE_BF16 - 1) + MOE_CHUNK
    rows = -(-rows // ROW_TILE_BF16) * ROW_TILE_BF16
    grid_spec = pltpu.PrefetchScalarGridSpec(
        num_scalar_prefetch=1,
        grid=(N // T, G, E // EPS_),
        in_specs=[
            pl.BlockSpec((T, D), lambda b, g, hf, cnt: (b, 0)),
            pl.BlockSpec((8, T), lambda b, g, hf, cnt: (0, b)),
            pl.BlockSpec((T, LANES), lambda b, g, hf, cnt: (b, 0)),
            pl.BlockSpec((None, None, EPS_, D, F), lambda b, g, hf, cnt: (l, g, hf, 0, 0)),
            pl.BlockSpec((None, None, EPS_, D, F), lambda b, g, hf, cnt: (l, g, hf, 0, 0)),
            pl.BlockSpec((None, None, EPS_, F, D), lambda b, g, hf, cnt: (l, g, hf, 0, 0)),
        ],
        out_specs=pl.BlockSpec((T, D), lambda b, g, hf, cnt: (b, 0)),
        scratch_shapes=[
            pltpu.VMEM((rows, D), BF16),
            pltpu.VMEM((rows, LANES), F32),
            pltpu.VMEM((rows, D), F32),
            pltpu.VMEM((SORT_WIN, D), BF16),
            pltpu.VMEM((SORT_WIN, LANES), F32),
        ],
    )
    return pl.pallas_call(
        functools.partial(_moe_kernel, T),
        grid_spec=grid_spec,
        out_shape=jax.ShapeDtypeStruct((N, D), BF16),
        compiler_params=_params(("arbitrary", "arbitrary", "arbitrary")),
        name="moe",
    )(counts, h3, rt, gcol, w1, w3, w2)


def _final_kernel(x_ref, y_ref, g_ref, o_ref):
    o_ref[...] = _rms(x_ref[...] + y_ref[...].astype(F32), g_ref[...])


def _final(x2, y, ln_final):
    N, D = x2.shape
    tm = TOKEN_TILE
    return pl.pallas_call(
        _final_kernel,
        grid=(N // tm,),
        in_specs=[pl.BlockSpec((tm, D), lambda i: (i, 0)), pl.BlockSpec((tm, D), lambda i: (i, 0)),
                  pl.BlockSpec((1, D), lambda i: (0, 0))],
        out_specs=pl.BlockSpec((tm, D), lambda i: (i, 0)),
        out_shape=jax.ShapeDtypeStruct((N, D), F32),
        compiler_params=_params(("arbitrary",)),
        name="final_norm",
    )(x2, y, ln_final.reshape(1, D))


def kernel(x, mem, ln_mix, w_in, conv_w, conv_b, i_bias, f_bias, ml_head_g, sb_head_g, w_out, ln_mem, ln_memkv,
           w_mq, w_mk, w_mv, w_mo, ln_ffn, w_rg, b_rg, w_re, b_re, w_e1, w_e3, w_e2, ln_final):
    B, S, D = x.shape
    N = B * S
    depth = w_in.shape[0]
    G, E = N_GROUPS, EXP_PER_GROUP
    assert S % TOKEN_TILE == 0 and TOKEN_TILE % ML_CHUNK == 0 and TOKEN_TILE % SORT_BLOCK == 0 and S % SB_SPAN == 0
    T = 2048 if S % 2048 == 0 else S

    c0, c1 = 4 * ML_WIDTH, 4 * ML_WIDTH + 2 * ML_HEADS
    w_ml = w_in[:, :, :c0].astype(BF16)
    w_sqt = jnp.swapaxes(w_in[:, :, c1:c1 + SB_WIDTH], 1, 2).astype(BF16)
    w_sk = w_in[:, :, c1 + SB_WIDTH:c1 + 2 * SB_WIDTH].astype(BF16)
    w_svt = jnp.swapaxes(w_in[:, :, c1 + 2 * SB_WIDTH:], 1, 2).astype(BF16)
    w_gate = w_in[:, :, c0:c1]
    w_g = jnp.pad(w_gate, ((0, 0), (0, 0), (0, LANES - 8))).astype(BF16)
    gate_b = jnp.concatenate([i_bias, f_bias], axis=-1)
    b_g = jnp.pad(gate_b, ((0, 0), (0, LANES - 8))).reshape(depth, 1, LANES)
    w_r = jnp.pad(jnp.concatenate([w_rg, w_re], axis=-1), ((0, 0), (0, 0), (0, LANES - G - G * E)))
    wr_hi = w_r.astype(BF16)
    wr_lo = (w_r - wr_hi.astype(F32)).astype(BF16)
    b_r = jnp.pad(jnp.concatenate([b_rg, b_re], axis=-1), ((0, 0), (0, LANES - G - G * E))).reshape(depth, 1, LANES)
    w_out_b, w_mq_b, w_mo_b = w_out.astype(BF16), w_mq.astype(BF16), w_mo.astype(BF16)
    w1, w3, w2 = w_e1.astype(BF16), w_e3.astype(BF16), w_e2.astype(BF16)
    r3 = lambda a: a.reshape(depth, 1, a.shape[-1])

    kx, vx = _memkv(mem, ln_memkv, w_mk.astype(BF16), w_mv.astype(BF16))

    x2, y = x.reshape(N, D), None
    nsub = TOKEN_TILE // SORT_BLOCK
    for l in range(depth):
        qka, vo, sqt, sk, svt, gc, gt = _proj(l, S, x2, y, r3(ln_mix), w_ml, w_sqt, w_sk, w_svt, w_g, b_g,
                                              conv_w, r3(conv_b))
        hml = _mlstm(l, B, S, qka, vo, gc, gt, r3(ml_head_g))
        hsb = _stickbreak(l, B, S, sqt, sk, svt, r3(sb_head_g))
        x2, h3, rt, gcol, cnt = _post(l, B, S, x2, y, hml, hsb, w_out_b, r3(ln_mem), w_mq_b, kx, vx, w_mo_b,
                                      r3(ln_ffn), wr_hi, wr_lo, b_r)
        counts = jnp.swapaxes(cnt[:, :G, :nsub], 1, 2).reshape(-1)
        y = _moe(l, T, counts, h3, rt, gcol, w1, w3, w2)
    return _final(x2, y, ln_final).reshape(B, S, D)
```

```python
import functools

import jax
import jax.numpy as jnp
from jax import lax
from jax.experimental import pallas as pl
from jax.experimental.pallas import tpu as pltpu

F32 = jnp.float32
BF16 = jnp.bfloat16
I32 = jnp.int32

ML_HEADS = 4
ML_DHEAD = 128
SB_HEADS = 4
SB_DHEAD = 128
ML_WIDTH = ML_HEADS * ML_DHEAD
SB_WIDTH = SB_HEADS * SB_DHEAD
CONV_W = 4
ML_CHUNK = 128
SB_SPAN = 256
XA_HEADS = 4
N_GROUPS = 4
EXP_PER_GROUP = 4
EPS = 1e-6
LOG2E = 1.4426950408889634

LANES = 128
ROW_TILE_BF16 = 16
SORT_BLOCK = 256
SORT_WIN = 384
SORT_BATCH = 4
MOE_CHUNK = 256
MOE_CHUNK_TAIL = 128
EXPERTS_PER_STEP = 2
TOKEN_TILE = 1024
VMEM_LIMIT = 56 * 1024 * 1024


def _dot(a, b):
    return jnp.dot(a, b, preferred_element_type=F32)


def _dot_nt(a, b):
    return lax.dot_general(a, b, (((1,), (1,)), ((), ())), preferred_element_type=F32)


def _split(x):
    hi = x.astype(BF16)
    lo = (x - hi.astype(F32)).astype(BF16)
    return hi, lo


def _rms(x, g):
    return x * lax.rsqrt(jnp.mean(x * x, axis=-1, keepdims=True) + EPS) * g


def _sigmoid(x):
    return 1.0 / (1.0 + jnp.exp(-x))


def _softplus(x):
    return jnp.maximum(x, 0.0) + jnp.log1p(jnp.exp(-jnp.abs(x)))


def _params(sem):
    return pltpu.CompilerParams(dimension_semantics=sem, vmem_limit_bytes=VMEM_LIMIT)


def _memkv_kernel(mem_ref, g_ref, wk_ref, wv_ref, k_ref, v_ref):
    h = _rms(mem_ref[0], g_ref[0]).astype(BF16)
    k_ref[0, 0] = _dot(h, wk_ref[0]).astype(BF16)
    v_ref[0, 0] = _dot(h, wv_ref[0]).astype(BF16)


def _memkv(mem, ln_memkv, w_mk, w_mv):
    B, M, D = mem.shape
    depth = w_mk.shape[0]
    return pl.pallas_call(
        _memkv_kernel,
        grid=(depth, B),
        in_specs=[
            pl.BlockSpec((1, M, D), lambda l, b: (b, 0, 0)),
            pl.BlockSpec((1, 1, D), lambda l, b: (l, 0, 0)),
            pl.BlockSpec((1, D, D), lambda l, b: (l, 0, 0)),
            pl.BlockSpec((1, D, D), lambda l, b: (l, 0, 0)),
        ],
        out_specs=[
            pl.BlockSpec((1, 1, M, D), lambda l, b: (l, b, 0, 0)),
            pl.BlockSpec((1, 1, M, D), lambda l, b: (l, b, 0, 0)),
        ],
        out_shape=[jax.ShapeDtypeStruct((depth, B, M, D), BF16)] * 2,
        compiler_params=_params(("arbitrary", "arbitrary")),
        name="memkv",
    )(mem, ln_memkv.reshape(depth, 1, D), w_mk, w_mv)


def _proj_kernel(has_y, tiles_per_seq, *refs):
    if has_y:
        x_ref, y_ref = refs[:2]
        refs = refs[2:]
        x = x_ref[...] + y_ref[...].astype(F32)
    else:
        x_ref = refs[0]
        refs = refs[1:]
        x = x_ref[...]
    (g_ref, wml_ref, wsqt_ref, wsk_ref, wsvt_ref, wg_ref, bg_ref, cw_ref, cb_ref,
     qka_ref, vo_ref, sqt_ref, sk_ref, svt_ref, gc_ref, gt_ref, qk_ref) = refs
    tm, W, L, P = x.shape[0], ML_WIDTH, ML_CHUNK, SB_SPAN

    @pl.when(pl.program_id(0) % tiles_per_seq == 0)
    def _():
        qk_ref[:8, :] = jnp.zeros((8, 2 * W), F32)

    h = _rms(x, g_ref[0]).astype(BF16)
    cw = cw_ref[0]
    cb = cb_ref[0]
    cblk = 2 * LANES

    def conv_block(c):
        rows = L
        for r in range(0, tm, rows):
            acc = cb[:, c:c + cblk]
            for tap in range(CONV_W):
                lo = 8 - (CONV_W - 1) + tap + r
                acc = acc + qk_ref[lo:lo + rows, c:c + cblk] * cw[tap:tap + 1, c:c + cblk]
            qa = acc * _sigmoid(acc)
            if c >= W:
                qa = qa * (ML_DHEAD ** -0.5)
            qka_ref[r:r + rows, c:c + cblk] = qa.astype(BF16)
        qk_ref[:8, c:c + cblk] = qk_ref[tm:, c:c + cblk]

    def store_v():
        vo_ref[:, :W] = _dot(h, wml_ref[0, :, 2 * W:3 * W]).astype(BF16)

    def store_o():
        vo_ref[:, W:] = _dot(h, wml_ref[0, :, 3 * W:]).astype(BF16)

    def store_sk():
        sk_ref[...] = _dot(h, wsk_ref[0]).astype(BF16)

    def store_sqt():
        sqt = (_dot_nt(wsqt_ref[0], h) * (SB_DHEAD ** -0.5 * LOG2E)).astype(BF16)
        for c in range(tm // P):
            sqt_ref[c] = sqt[:, c * P:(c + 1) * P]

    def store_svt():
        svt = _dot_nt(wsvt_ref[0], h).astype(BF16)
        for c in range(tm // P):
            svt_ref[c] = svt[:, c * P:(c + 1) * P]

    others = [store_v, store_o, store_sk, store_sqt, store_svt]
    blocks = list(range(0, 2 * W, cblk))
    for n, c in enumerate(blocks):
        qk_ref[8:, c:c + cblk] = _dot(h, wml_ref[0, :, c:c + cblk])
        if n > 0:
            conv_block(blocks[n - 1])
        if others:
            others.pop(0)()
    conv_block(blocks[-1])
    for rest in others:
        rest()
    gc = _dot(h, wg_ref[0]) + bg_ref[0]
    gc_ref[...] = gc
    gt = gc.T
    for c in range(tm // L):
        gt_ref[c] = gt[:8, c * L:(c + 1) * L]


def _proj(l, S, x2, y, ln_mix, w_ml, w_sqt, w_sk, w_svt, w_g, b_g, conv_w, conv_b):
    N, D = x2.shape
    tm, W, L, P, SW = TOKEN_TILE, ML_WIDTH, ML_CHUNK, SB_SPAN, SB_WIDTH
    row = lambda i: (i, 0)
    lay = lambda i: (l, 0, 0)
    ins = [x2] + ([y] if y is not None else [])
    in_specs = [pl.BlockSpec((tm, D), row)] * len(ins) + [
        pl.BlockSpec((1, 1, D), lay),
        pl.BlockSpec((1, D, 4 * W), lay),
        pl.BlockSpec((1, SW, D), lay),
        pl.BlockSpec((1, D, SW), lay),
        pl.BlockSpec((1, SW, D), lay),
        pl.BlockSpec((1, D, LANES), lay),
        pl.BlockSpec((1, 1, LANES), lay),
        pl.BlockSpec((1, CONV_W, 2 * W), lay),
        pl.BlockSpec((1, 1, 2 * W), lay),
    ]
    return pl.pallas_call(
        functools.partial(_proj_kernel, y is not None, S // tm),
        grid=(N // tm,),
        in_specs=in_specs,
        out_specs=[
            pl.BlockSpec((tm, 2 * W), row),
            pl.BlockSpec((tm, 2 * W), row),
            pl.BlockSpec((tm // P, SW, P), lambda i: (i, 0, 0)),
            pl.BlockSpec((tm, SW), row),
            pl.BlockSpec((tm // P, SW, P), lambda i: (i, 0, 0)),
            pl.BlockSpec((tm, LANES), row),
            pl.BlockSpec((tm // L, 8, L), lambda i: (i, 0, 0)),
        ],
        out_shape=[
            jax.ShapeDtypeStruct((N, 2 * W), BF16),
            jax.ShapeDtypeStruct((N, 2 * W), BF16),
            jax.ShapeDtypeStruct((N // P, SW, P), BF16),
            jax.ShapeDtypeStruct((N, SW), BF16),
            jax.ShapeDtypeStruct((N // P, SW, P), BF16),
            jax.ShapeDtypeStruct((N, LANES), F32),
            jax.ShapeDtypeStruct((N // L, 8, L), F32),
        ],
        scratch_shapes=[pltpu.VMEM((8 + tm, 2 * W), F32)],
        compiler_params=_params(("arbitrary",)),
        name="proj",
    )(*ins, ln_mix, w_ml, w_sqt, w_sk, w_svt, w_g, b_g, conv_w, conv_b)


def _mlstm_kernel(qk_ref, vo_ref, gc_ref, gt_ref, hg_ref, out_ref, s_ref, m_ref):
    L, d, W, H = ML_CHUNK, ML_DHEAD, ML_WIDTH, ML_HEADS
    nb = qk_ref.shape[0]

    @pl.when(pl.program_id(1) == 0)
    def _():
        s_ref[...] = jnp.zeros_like(s_ref)
        m_ref[...] = jnp.zeros_like(m_ref)

    row = lax.broadcasted_iota(I32, (L, L), 0)
    col = lax.broadcasted_iota(I32, (L, L), 1)
    causal = col <= row
    tri = jnp.where(causal, 1.0, 0.0).astype(BF16)
    tri_t = jnp.where(row <= col, 1.0, 0.0).astype(BF16)
    lane = lax.broadcasted_iota(I32, (L, LANES), 1)
    ones_cols = jnp.ones((L, d), BF16)
    ones_sq = jnp.ones((d, d), BF16)
    r2 = lax.broadcasted_iota(I32, (LANES, 2 * d), 0)
    c2 = lax.broadcasted_iota(I32, (LANES, 2 * d), 1)
    pick = [jnp.where(((c2 < d) & (r2 == H + h)) | ((c2 >= d) & (r2 == h)), 1.0, 0.0).astype(BF16)
            for h in range(H)]
    hg = hg_ref[0]
    units = [(bb, h) for bb in range(nb) for h in range(H)]

    gts, bts, xs = [], [], []
    for bb in range(nb):
        gt = gt_ref[bb, 0]
        gc = gc_ref[bb]
        lsc_hi, lsc_lo = _split(-_softplus(-gc))
        bc = _dot(tri, lsc_hi) + _dot(tri, lsc_lo)
        lst_hi, lst_lo = _split(-_softplus(-gt))
        gts.append(gt)
        bts.append(_dot(lst_hi, tri_t) + _dot(lst_lo, tri_t))
        xs.append(_split(jnp.where(lane < H, gc, bc)))

    qbs, kbs, v1s, qks, carried, spread = [], [], [], [], [], []
    for bb, h in units:
        qb = qk_ref[bb, :, h * d:(h + 1) * d]
        kb = qk_ref[bb, :, W + h * d:W + (h + 1) * d]
        v = vo_ref[bb, :, h * d:(h + 1) * d]
        qbs.append(qb)
        kbs.append(kb)
        v1s.append(jnp.concatenate([v, ones_cols], axis=1))
        qks.append(_dot_nt(qb, kb))
        carried.append(_dot(qb, s_ref[bb * H + h].astype(BF16)))
        spread.append(_dot(xs[bb][0], pick[h]) + _dot(xs[bb][1], pick[h]))

    ss, m_ts, w_inters = [], [], []
    for u, (bb, h) in enumerate(units):
        b_c = spread[u][:, :d]
        i_r = gts[bb][h:h + 1, :]
        b_r = bts[bb][H + h:H + h + 1, :]
        logd = jnp.where(causal, b_c - b_r + i_r, -jnp.inf)
        inter = b_c + m_ref[u][:1, :]
        m_t = jnp.maximum(inter, jnp.max(logd, axis=-1, keepdims=True))
        ss.append((qks[u] * jnp.exp(logd - m_t)).astype(BF16))
        m_ts.append(m_t)
        w_inters.append(jnp.exp(inter - m_t))

    intras = [_dot(ss[u], v1s[u]) for u in range(len(units))]

    hhs = []
    for u in range(len(units)):
        w2 = jnp.concatenate([w_inters[u], w_inters[u]], axis=1)
        tot = intras[u] + w2 * carried[u]
        hhs.append(tot[:, :d] / jnp.maximum(jnp.abs(tot[:, d:]), jnp.exp(-m_ts[u])))
    sqs = [_dot((hh * hh).astype(BF16), ones_sq) * (1.0 / d) for hh in hhs]
    for u, (bb, h) in enumerate(units):
        og = vo_ref[bb, :, W + h * d:W + (h + 1) * d].astype(F32)
        hn = hhs[u] * lax.rsqrt(sqs[u] + EPS) * hg[:, h * d:(h + 1) * d]
        out_ref[bb, :, h * d:(h + 1) * d] = (_sigmoid(og) * hn).astype(BF16)

    kws, w_olds, m_news = [], [], []
    for u, (bb, h) in enumerate(units):
        b_c, i_c = spread[u][:, :d], spread[u][:, d:]
        i_r = gts[bb][h:h + 1, :]
        b_r = bts[bb][H + h:H + h + 1, :]
        m_prev = m_ref[u][:1, :1]
        b_last = b_r[:, L - 1:L]
        m_new = jnp.maximum(b_last + m_prev, jnp.max(b_last - b_r + i_r, axis=-1, keepdims=True))
        w_olds.append(jnp.exp(b_last + m_prev - m_new))
        m_news.append(m_new)
        kws.append((kbs[u].astype(F32) * jnp.exp(b_last - b_c + i_c - m_new)).T.astype(BF16))
    for u in range(len(units)):
        s_ref[u] = w_olds[u] * s_ref[u] + _dot(kws[u], v1s[u])
        m_ref[u] = jnp.broadcast_to(m_news[u], (8, LANES))


def _mlstm(l, B, S, qka, vo, gc, gt, ml_head_g):
    L, W, H, d = ML_CHUNK, ML_WIDTH, ML_HEADS, ML_DHEAD
    nc = S // L
    nb = 4 if B % 4 == 0 else (2 if B % 2 == 0 else 1)
    out = pl.pallas_call(
        _mlstm_kernel,
        grid=(B // nb, nc),
        in_specs=[
            pl.BlockSpec((nb, L, 2 * W), lambda b, c: (b, c, 0)),
            pl.BlockSpec((nb, L, 2 * W), lambda b, c: (b, c, 0)),
            pl.BlockSpec((nb, L, LANES), lambda b, c: (b, c, 0)),
            pl.BlockSpec((nb, 1, 8, L), lambda b, c: (b, c, 0, 0)),
            pl.BlockSpec((1, 1, W), lambda b, c: (l, 0, 0)),
        ],
        out_specs=pl.BlockSpec((nb, L, W), lambda b, c: (b, c, 0)),
        out_shape=jax.ShapeDtypeStruct((B, S, W), BF16),
        scratch_shapes=[
            pltpu.VMEM((nb * H, d, 2 * d), F32),
            pltpu.VMEM((nb * H, 8, LANES), F32),
        ],
        compiler_params=_params(("arbitrary", "arbitrary")),
        name="mlstm",
    )(qka.reshape(B, S, 2 * W), vo.reshape(B, S, 2 * W), gc.reshape(B, S, LANES), gt.reshape(B, nc, 8, L),
      ml_head_g)
    return out.reshape(B * S, W)


def _sb_kernel(qt_ref, k_ref, vt_ref, g_ref, o_ref, acc_ref):
    P, H, d = SB_SPAN, SB_HEADS, SB_DHEAD
    i = pl.program_id(1)
    row = lax.broadcasted_iota(I32, (P, P), 0)
    col = lax.broadcasted_iota(I32, (P, P), 1)
    strict = row < col
    neg_from = jnp.where(col >= row, -1.0, 0.0).astype(BF16)

    heads = [slice(h * d, (h + 1) * d) for h in range(H)]

    def spans(js, csums, diagonal):
        units = [(n, j, h) for n, j in enumerate(js) for h in range(H)]
        masked = [diagonal and n == 0 for n, _, _ in units]
        ks = [k_ref[pl.ds(pl.multiple_of(j * P, P), P), heads[h]] for _, j, h in units]
        qts = [qt_ref[0, heads[h], :] for _, _, h in units]
        zs = [_dot(kj, qt) for kj, qt in zip(ks, qts)]
        sps = []
        for z, m in zip(zs, masked):
            sp = jnp.maximum(z, 0.0) + jnp.log(1.0 + jnp.exp2(-jnp.abs(z))) * LOG2E
            sps.append(jnp.where(strict, sp, 0.0) if m else sp)
        es = [z + _dot(neg_from, sp.astype(BF16)) for z, sp in zip(zs, sps)]
        csums = list(csums) if csums is not None else [None] * H
        parts = [None] * H
        for u, (_, j, h) in enumerate(units):
            csum = jnp.sum(sps[u], axis=0, keepdims=True)
            if masked[u]:
                a = jnp.where(strict, jnp.exp2(es[u]), 0.0)
                csums[h] = csum
            else:
                a = jnp.exp2(es[u] - csums[h])
                csums[h] = csums[h] + csum
            part = _dot(vt_ref[j, heads[h], :], a.astype(BF16))
            parts[h] = part if parts[h] is None else parts[h] + part
        for h in range(H):
            if diagonal:
                acc_ref[heads[h], :] = parts[h]
            else:
                acc_ref[heads[h], :] += parts[h]
        return tuple(csums)

    @pl.when(i == 0)
    def _():
        spans([i], None, True)

    @pl.when(i > 0)
    def _():
        csums = spans([i, i - 1], None, True)
        left = i - 1
        odd = left % 2
        csums = lax.fori_loop(0, odd, lambda jj, c: spans([i - 2], c, False), csums)
        lax.fori_loop(0, left // 2, lambda p, c: spans([i - 2 - odd - 2 * p, i - 3 - odd - 2 * p], c, False), csums)
    g = g_ref[0]
    for h in range(H):
        hs = slice(h * d, (h + 1) * d)
        acc = acc_ref[hs, :]
        on = acc * lax.rsqrt(jnp.mean(acc * acc, axis=0, keepdims=True) + EPS)
        o_ref[:, hs] = (on.T * g[:, hs]).astype(BF16)


def _stickbreak(l, B, S, sqt, sk, svt, sb_head_g):
    P, W = SB_SPAN, SB_WIDTH
    nq = S // P
    return pl.pallas_call(
        _sb_kernel,
        grid=(B, nq),
        in_specs=[
            pl.BlockSpec((1, W, P), lambda b, i: (b * nq + i, 0, 0)),
            pl.BlockSpec((S, W), lambda b, i: (b, 0)),
            pl.BlockSpec((nq, W, P), lambda b, i: (b, 0, 0)),
            pl.BlockSpec((1, 1, W), lambda b, i: (l, 0, 0)),
        ],
        out_specs=pl.BlockSpec((P, W), lambda b, i: (b * nq + i, 0)),
        out_shape=jax.ShapeDtypeStruct((B * S, W), BF16),
        scratch_shapes=[pltpu.VMEM((W, P), F32)],
        compiler_params=_params(("arbitrary", "arbitrary")),
        name="stickbreak",
    )(sqt, sk, svt, sb_head_g)


def _route(lgt, tm):
    G, E = N_GROUPS, EXP_PER_GROUP
    lg = [lgt[i:i + 1, :] for i in range(G + G * E)]
    mg = functools.reduce(jnp.maximum, lg[:G])
    eg = [jnp.exp(v - mg) for v in lg[:G]]
    zg = functools.reduce(jnp.add, eg)
    pg = [v / zg for v in eg]
    p_sel = functools.reduce(jnp.maximum, pg)
    gid = jnp.where(pg[0] == p_sel, 0, jnp.where(pg[1] == p_sel, 1, jnp.where(pg[2] == p_sel, 2, 3)))
    es = [jnp.where(gid == 0, lg[G + e], jnp.where(gid == 1, lg[G + E + e],
          jnp.where(gid == 2, lg[G + 2 * E + e], lg[G + 3 * E + e]))) for e in range(E)]
    me = functools.reduce(jnp.maximum, es)
    ee = [jnp.exp(v - me) for v in es]
    ze = functools.reduce(jnp.add, ee)
    pe = [v / ze for v in ee]

    def first_max(vals):
        top = functools.reduce(jnp.maximum, vals)
        idx = jnp.where(vals[0] == top, 0, jnp.where(vals[1] == top, 1, jnp.where(vals[2] == top, 2, 3)))
        return top, idx

    v1, i1 = first_max(pe)
    v2, i2 = first_max([jnp.where(i1 == e, -1.0, pe[e]) for e in range(E)])
    tsum = v1 + v2
    tw1, tw2 = v1 / tsum, v2 / tsum
    gates = [p_sel * (jnp.where(i1 == e, tw1, 0.0) + jnp.where(i2 == e, tw2, 0.0)) for e in range(E)]

    sub8 = lax.broadcasted_iota(I32, (8, tm), 0)
    onehot = jnp.where(sub8 == gid, 1.0, 0.0)
    r = lax.broadcasted_iota(I32, (SORT_BLOCK, SORT_BLOCK), 0)
    c = lax.broadcasted_iota(I32, (SORT_BLOCK, SORT_BLOCK), 1)
    before = jnp.where(r < c, 1.0, 0.0).astype(BF16)
    lane = lax.broadcasted_iota(I32, (8, LANES), 1)
    counts = jnp.zeros((8, LANES), F32)
    lps = []
    for jj in range(tm // SORT_BLOCK):
        oj = onehot[:, jj * SORT_BLOCK:(jj + 1) * SORT_BLOCK]
        pre = _dot(oj.astype(BF16), before)
        cnt = jnp.sum(oj, axis=-1, keepdims=True)
        pcnt = jnp.floor((cnt + (ROW_TILE_BF16 - 1.0)) * (1.0 / ROW_TILE_BF16)) * ROW_TILE_BF16
        start = jnp.zeros((1, 1), F32)
        lp = jnp.zeros((1, SORT_BLOCK), F32)
        for g in range(G):
            lp = lp + oj[g:g + 1, :] * (pre[g:g + 1, :] + start)
            start = start + pcnt[g:g + 1, :]
        lps.append(lp)
        counts = counts + jnp.where(lane == jj, cnt, 0.0)
    lp = jnp.concatenate(lps, axis=1)
    rows = gates + [gid.astype(F32), lp]
    pack = jnp.zeros((8, tm), F32)
    for kk, v in enumerate(rows):
        pack = pack + jnp.where(sub8 == kk, v, 0.0)
    return pack, counts.astype(I32)


def _post_kernel(has_y, *refs):
    if has_y:
        x_ref, y_ref = refs[:2]
        refs = refs[2:]
        x = x_ref[...] + y_ref[...].astype(F32)
    else:
        x_ref = refs[0]
        refs = refs[1:]
        x = x_ref[...]
    (hml_ref, hsb_ref, wo_ref, gmem_ref, wq_ref, k_ref, v_ref, wmo_ref, gffn_ref, wrh_ref, wrl_ref, br_ref,
     x2_ref, h3_ref, rt_ref, gcol_ref, cnt_ref) = refs
    tm = x.shape[0]
    x1 = x + _dot(hml_ref[...], wo_ref[0, :ML_WIDTH, :]) + _dot(hsb_ref[...], wo_ref[0, ML_WIDTH:, :])
    q = _dot(_rms(x1, gmem_ref[0]).astype(BF16), wq_ref[0]).astype(BF16)
    kk = k_ref[0, 0]
    vv = v_ref[0, 0]
    dh = q.shape[1] // XA_HEADS
    heads = [slice(hd * dh, (hd + 1) * dh) for hd in range(XA_HEADS)]
    scores = [_dot_nt(q[:, sl], kk[:, sl]) * (dh ** -0.5) for sl in heads]
    probs = []
    for s in scores:
        p = jnp.exp(s - jnp.max(s, axis=-1, keepdims=True))
        probs.append((p / jnp.sum(p, axis=-1, keepdims=True)).astype(BF16))
    outs = [_dot(p, vv[:, sl]).astype(BF16) for p, sl in zip(probs, heads)]
    x2 = x1 + _dot(jnp.concatenate(outs, axis=-1), wmo_ref[0])
    x2_ref[...] = x2
    h3_hi, h3_lo = _split(_rms(x2, gffn_ref[0]))
    h3_ref[...] = h3_hi
    wrh = wrh_ref[0]
    both = _dot(h3_hi, jnp.concatenate([wrh, wrl_ref[0]], axis=1))
    lg = both[:, :LANES] + both[:, LANES:] + _dot(h3_lo, wrh) + br_ref[0]
    pack, counts = _route(lg.T, tm)
    rt_ref[...] = pack
    gcol_ref[...] = jnp.concatenate([pack, jnp.zeros((LANES - 8, tm), F32)], axis=0).T
    cnt_ref[0] = counts


def _post(l, B, S, x2, y, hml, hsb, w_out, ln_mem, w_mq, kx, vx, w_mo, ln_ffn, wr_hi, wr_lo, b_r):
    N, D = x2.shape
    M = kx.shape[2]
    tm = TOKEN_TILE
    tpb = S // tm
    row = lambda i: (i, 0)
    lay = lambda i: (l, 0, 0)
    ins = [x2] + ([y] if y is not None else [])
    in_specs = [pl.BlockSpec((tm, D), row)] * len(ins) + [
        pl.BlockSpec((tm, ML_WIDTH), row),
        pl.BlockSpec((tm, SB_WIDTH), row),
        pl.BlockSpec((1, D, D), lay),
        pl.BlockSpec((1, 1, D), lay),
        pl.BlockSpec((1, D, D), lay),
        pl.BlockSpec((1, 1, M, D), lambda i: (l, i // tpb, 0, 0)),
        pl.BlockSpec((1, 1, M, D), lambda i: (l, i // tpb, 0, 0)),
        pl.BlockSpec((1, D, D), lay),
        pl.BlockSpec((1, 1, D), lay),
        pl.BlockSpec((1, D, LANES), lay),
        pl.BlockSpec((1, D, LANES), lay),
        pl.BlockSpec((1, 1, LANES), lay),
    ]
    return pl.pallas_call(
        functools.partial(_post_kernel, y is not None),
        grid=(N // tm,),
        in_specs=in_specs,
        out_specs=[
            pl.BlockSpec((tm, D), row),
            pl.BlockSpec((tm, D), row),
            pl.BlockSpec((8, tm), lambda i: (0, i)),
            pl.BlockSpec((tm, LANES), row),
            pl.BlockSpec((1, 8, LANES), lambda i: (i, 0, 0)),
        ],
        out_shape=[
            jax.ShapeDtypeStruct((N, D), F32),
            jax.ShapeDtypeStruct((N, D), BF16),
            jax.ShapeDtypeStruct((8, N), F32),
            jax.ShapeDtypeStruct((N, LANES), F32),
            jax.ShapeDtypeStruct((N // tm, 8, LANES), I32),
        ],
        compiler_params=_params(("arbitrary",)),
        name="post",
    )(*ins, hml, hsb, w_out, ln_mem, w_mq, kx, vx, w_mo, ln_ffn, wr_hi, wr_lo, b_r)


def _moe_kernel(T, cnt_ref, h3_ref, rt_ref, gcol_ref, w1_ref, w3_ref, w2_ref, y_ref,
                xs_ref, gs_ref, ys_ref, sl_ref, gl_ref):
    G, CH, CH_TAIL, SBK, WIN, RT = N_GROUPS, MOE_CHUNK, MOE_CHUNK_TAIL, SORT_BLOCK, SORT_WIN, ROW_TILE_BF16
    b, g, hf = pl.program_id(0), pl.program_id(1), pl.program_id(2)
    nsb = T // SBK
    nbat = sl_ref.shape[0]
    n_half = pl.num_programs(2)
    D = h3_ref.shape[1]

    pc = [[((cnt_ref[(b * nsb + j) * G + gg] + (RT - 1)) // RT) * RT for j in range(nsb)] for gg in range(G)]
    base = [0]
    for gg in range(G):
        base.append(base[-1] + functools.reduce(lambda u, w: u + w, pc[gg]))

    def seg_starts(j):
        out, src = [], 0
        for gg in range(G):
            dst = base[gg]
            for jp in range(j):
                dst = dst + pc[gg][jp]
            out.append((src, dst))
            src = src + pc[gg][j]
        return out

    @pl.when((g == 0) & (hf == 0))
    def _sort():
        total = pl.multiple_of(base[G], RT)
        xs_ref[pl.ds(total, CH), :] = jnp.zeros((CH, D), BF16)
        gs_ref[pl.ds(total, CH), :] = jnp.zeros((CH, LANES), F32)
        riota = lax.broadcasted_iota(I32, (WIN, SBK), 0)
        for j0 in range(0, nsb, nbat):
            for jj in range(nbat):
                j = j0 + jj
                lp = rt_ref[5:6, j * SBK:(j + 1) * SBK].astype(I32)
                perm = jnp.where(riota == lp, 1.0, 0.0).astype(BF16)
                sl_ref[jj] = _dot(perm, h3_ref[j * SBK:(j + 1) * SBK, :]).astype(BF16)
                g_hi, g_lo = _split(gcol_ref[j * SBK:(j + 1) * SBK, :])
                both = _dot(perm, jnp.concatenate([g_hi, g_lo], axis=1))
                gl_ref[jj] = both[:, :LANES] + both[:, LANES:]
            for jj in range(nbat):
                for gg, (src, dst) in enumerate(seg_starts(j0 + jj)):
                    def copy_in(c, carry, jj=jj, src=src, dst=dst):
                        s = pl.multiple_of(src + c * RT, RT)
                        t = pl.multiple_of(dst + c * RT, RT)
                        xs_ref[pl.ds(t, RT), :] = sl_ref[jj, pl.ds(s, RT), :]
                        gs_ref[pl.ds(t, RT), :] = gl_ref[jj, pl.ds(s, RT), :]
                        return carry
                    lax.fori_loop(0, pc[gg][j0 + jj] // RT, copy_in, 0)

    rows_g = base[1] - base[0]
    base_g = base[0]
    for gg in range(1, G):
        rows_g = jnp.where(g == gg, base[gg + 1] - base[gg], rows_g)
        base_g = jnp.where(g == gg, base[gg], base_g)

    def run_chunk(r0, rows):
        r0 = pl.multiple_of(r0, RT)
        xc = xs_ref[pl.ds(r0, rows), :]
        gc = gs_ref[pl.ds(r0, rows), :]
        ups = [(_dot(xc, w1_ref[e]), _dot(xc, w3_ref[e])) for e in range(EXPERTS_PER_STEP)]
        hms = []
        for e, (a, u) in enumerate(ups):
            gate = gc[:, e:e + 1]
            for k in range(1, EXP_PER_GROUP // EXPERTS_PER_STEP):
                gate = jnp.where(hf == k, gc[:, k * EXPERTS_PER_STEP + e:k * EXPERTS_PER_STEP + e + 1], gate)
            hms.append(((a * _sigmoid(a)) * u * gate).astype(BF16))
        acc = functools.reduce(jnp.add, [_dot(hm, w2_ref[e]) for e, hm in enumerate(hms)])

        @pl.when(hf == 0)
        def _():
            ys_ref[pl.ds(r0, rows), :] = acc

        @pl.when(hf != 0)
        def _():
            ys_ref[pl.ds(r0, rows), :] += acc

    start, left = base_g, rows_g
    for size in (2 * CH, CH):
        n = left // size

        def body(c, carry, start=start, size=size):
            run_chunk(start + c * size, size)
            return carry

        lax.fori_loop(0, n, body, 0)
        start, left = start + n * size, left - n * size

    def tail_body(c, carry, start=start):
        run_chunk(start + c * CH_TAIL, CH_TAIL)
        return carry

    lax.fori_loop(0, (left + (CH_TAIL - 1)) // CH_TAIL, tail_body, 0)

    @pl.when((g == G - 1) & (hf == n_half - 1))
    def _unsort():
        ciota = lax.broadcasted_iota(I32, (SBK, WIN), 1)
        for j0 in range(0, nsb, nbat):
            for jj in range(nbat):
                for gg, (src, dst) in enumerate(seg_starts(j0 + jj)):
                    def copy_out(c, carry, jj=jj, src=src, dst=dst):
                        s = pl.multiple_of(src + c * RT, RT)
                        t = pl.multiple_of(dst + c * RT, RT)
                        sl_ref[jj, pl.ds(s, RT), :] = ys_ref[pl.ds(t, RT), :].astype(BF16)
                        return carry
                    lax.fori_loop(0, pc[gg][j0 + jj] // RT, copy_out, 0)
            for jj in range(nbat):
                j = j0 + jj
                lp = gcol_ref[j * SBK:(j + 1) * SBK, 5:6].astype(I32)
                perm_t = jnp.where(ciota == lp, 1.0, 0.0).astype(BF16)
                y_ref[j * SBK:(j + 1) * SBK, :] = _dot(perm_t, sl_ref[jj]).astype(BF16)


def _moe(l, T, counts, h3, rt, gcol, w1, w3, w2):
    N, D = h3.shape
    G, E, EPS_ = N_GROUPS, EXP_PER_GROUP, EXPERTS_PER_STEP
    F = w1.shape[-1]
    rows = T + (T // SORT_BLOCK) * G * (ROW_TILE_BF16 - 1) + MOE_CHUNK
    rows = -(-rows // ROW_TILE_BF16) * ROW_TILE_BF16
    nsb = T // SORT_BLOCK
    nbat = next(n for n in (SORT_BATCH, 2, 1) if nsb % n == 0)
    grid_spec = pltpu.PrefetchScalarGridSpec(
        num_scalar_prefetch=1,
        grid=(N // T, G, E // EPS_),
        in_specs=[
            pl.BlockSpec((T, D), lambda b, g, hf, cnt: (b, 0)),
            pl.BlockSpec((8, T), lambda b, g, hf, cnt: (0, b)),
            pl.BlockSpec((T, LANES), lambda b, g, hf, cnt: (b, 0)),
            pl.BlockSpec((None, None, EPS_, D, F), lambda b, g, hf, cnt: (l, g, hf, 0, 0)),
            pl.BlockSpec((None, None, EPS_, D, F), lambda b, g, hf, cnt: (l, g, hf, 0, 0)),
            pl.BlockSpec((None, None, EPS_, F, D), lambda b, g, hf, cnt: (l, g, hf, 0, 0)),
        ],
        out_specs=pl.BlockSpec((T, D), lambda b, g, hf, cnt: (b, 0)),
        scratch_shapes=[
            pltpu.VMEM((rows, D), BF16),
            pltpu.VMEM((rows, LANES), F32),
            pltpu.VMEM((rows, D), F32),
            pltpu.VMEM((nbat, SORT_WIN, D), BF16),
            pltpu.VMEM((nbat, SORT_WIN, LANES), F32),
        ],
    )
    return pl.pallas_call(
        functools.partial(_moe_kernel, T),
        grid_spec=grid_spec,
        out_shape=jax.ShapeDtypeStruct((N, D), BF16),
        compiler_params=_params(("arbitrary", "arbitrary", "arbitrary")),
        name="moe",
    )(counts, h3, rt, gcol, w1, w3, w2)


def _final_kernel(x_ref, y_ref, g_ref, o_ref):
    o_ref[...] = _rms(x_ref[...] + y_ref[...].astype(F32), g_ref[...])


def _final(x2, y, ln_final):
    N, D = x2.shape
    tm = TOKEN_TILE
    return pl.pallas_call(
        _final_kernel,
        grid=(N // tm,),
        in_specs=[pl.BlockSpec((tm, D), lambda i: (i, 0)), pl.BlockSpec((tm, D), lambda i: (i, 0)),
                  pl.BlockSpec((1, D), lambda i: (0, 0))],
        out_specs=pl.BlockSpec((tm, D), lambda i: (i, 0)),
        out_shape=jax.ShapeDtypeStruct((N, D), F32),
        compiler_params=_params(("arbitrary",)),
        name="final_norm",
    )(x2, y, ln_final.reshape(1, D))


def kernel(x, mem, ln_mix, w_in, conv_w, conv_b, i_bias, f_bias, ml_head_g, sb_head_g, w_out, ln_mem, ln_memkv,
           w_mq, w_mk, w_mv, w_mo, ln_ffn, w_rg, b_rg, w_re, b_re, w_e1, w_e3, w_e2, ln_final):
    B, S, D = x.shape
    N = B * S
    depth = w_in.shape[0]
    G, E = N_GROUPS, EXP_PER_GROUP
    assert S % TOKEN_TILE == 0 and TOKEN_TILE % ML_CHUNK == 0 and TOKEN_TILE % SORT_BLOCK == 0 and S % SB_SPAN == 0
    T = 2048 if S % 2048 == 0 else S

    c0, c1 = 4 * ML_WIDTH, 4 * ML_WIDTH + 2 * ML_HEADS
    w_ml = w_in[:, :, :c0].astype(BF16)
    w_sqt = jnp.swapaxes(w_in[:, :, c1:c1 + SB_WIDTH], 1, 2).astype(BF16)
    w_sk = w_in[:, :, c1 + SB_WIDTH:c1 + 2 * SB_WIDTH].astype(BF16)
    w_svt = jnp.swapaxes(w_in[:, :, c1 + 2 * SB_WIDTH:], 1, 2).astype(BF16)
    w_gate = w_in[:, :, c0:c1]
    w_g = jnp.pad(w_gate, ((0, 0), (0, 0), (0, LANES - 8))).astype(BF16)
    gate_b = jnp.concatenate([i_bias, f_bias], axis=-1)
    b_g = jnp.pad(gate_b, ((0, 0), (0, LANES - 8))).reshape(depth, 1, LANES)
    w_r = jnp.pad(jnp.concatenate([w_rg, w_re], axis=-1), ((0, 0), (0, 0), (0, LANES - G - G * E)))
    wr_hi = w_r.astype(BF16)
    wr_lo = (w_r - wr_hi.astype(F32)).astype(BF16)
    b_r = jnp.pad(jnp.concatenate([b_rg, b_re], axis=-1), ((0, 0), (0, LANES - G - G * E))).reshape(depth, 1, LANES)
    w_out_b, w_mq_b, w_mo_b = w_out.astype(BF16), w_mq.astype(BF16), w_mo.astype(BF16)
    w1, w3, w2 = w_e1.astype(BF16), w_e3.astype(BF16), w_e2.astype(BF16)
    r3 = lambda a: a.reshape(depth, 1, a.shape[-1])

    kx, vx = _memkv(mem, ln_memkv, w_mk.astype(BF16), w_mv.astype(BF16))

    x2, y = x.reshape(N, D), None
    nsub = TOKEN_TILE // SORT_BLOCK
    for l in range(depth):
        qka, vo, sqt, sk, svt, gc, gt = _proj(l, S, x2, y, r3(ln_mix), w_ml, w_sqt, w_sk, w_svt, w_g, b_g,
                                              conv_w, r3(conv_b))
        hml = _mlstm(l, B, S, qka, vo, gc, gt, r3(ml_head_g))
        hsb = _stickbreak(l, B, S, sqt, sk, svt, r3(sb_head_g))
        x2, h3, rt, gcol, cnt = _post(l, B, S, x2, y, hml, hsb, w_out_b, r3(ln_mem), w_mq_b, kx, vx, w_mo_b,
                                      r3(ln_ffn), wr_hi, wr_lo, b_r)
        counts = jnp.swapaxes(cnt[:, :G, :nsub], 1, 2).reshape(-1)
        y = _moe(l, T, counts, h3, rt, gcol, w1, w3, w2)
    return _final(x2, y, ln_final).reshape(B, S, D)
```

```python
import functools

import jax
import jax.numpy as jnp
from jax import lax
from jax.experimental import pallas as pl
from jax.experimental.pallas import tpu as pltpu

F32 = jnp.float32
BF16 = jnp.bfloat16
I32 = jnp.int32

ML_HEADS = 4
ML_DHEAD = 128
SB_HEADS = 4
SB_DHEAD = 128
ML_WIDTH = ML_HEADS * ML_DHEAD
SB_WIDTH = SB_HEADS * SB_DHEAD
CONV_W = 4
ML_CHUNK = 128
SB_SPAN = 256
XA_HEADS = 4
N_GROUPS = 4
EXP_PER_GROUP = 4
EPS = 1e-6
LOG2E = 1.4426950408889634
SP_LINEAR = 100.0

LANES = 128
ROW_TILE_BF16 = 16
SORT_BLOCK = 256
SORT_WIN = 384
SORT_BATCH = 4
MOE_CHUNK = 256
MOE_CHUNK_TAIL = 128
EXPERTS_PER_STEP = 2
TOKEN_TILE = 1024
VMEM_LIMIT = 56 * 1024 * 1024


def _dot(a, b):
    return jnp.dot(a, b, preferred_element_type=F32)


def _dot_nt(a, b):
    return lax.dot_general(a, b, (((1,), (1,)), ((), ())), preferred_element_type=F32)


def _split(x):
    hi = x.astype(BF16)
    lo = (x - hi.astype(F32)).astype(BF16)
    return hi, lo


def _rms(x, g):
    return x * lax.rsqrt(jnp.mean(x * x, axis=-1, keepdims=True) + EPS) * g


def _sigmoid(x):
    return 0.5 * jnp.tanh(0.5 * x) + 0.5


def _softplus(x):
    return jnp.maximum(x, 0.0) + jnp.log1p(jnp.exp(-jnp.abs(x)))


def _params(sem):
    return pltpu.CompilerParams(dimension_semantics=sem, vmem_limit_bytes=VMEM_LIMIT)


def _memkv_kernel(mem_ref, g_ref, wk_ref, wv_ref, k_ref, v_ref):
    h = _rms(mem_ref[0], g_ref[0]).astype(BF16)
    k_ref[0, 0] = _dot(h, wk_ref[0]).astype(BF16)
    v_ref[0, 0] = _dot(h, wv_ref[0]).astype(BF16)


def _memkv(mem, ln_memkv, w_mk, w_mv):
    B, M, D = mem.shape
    depth = w_mk.shape[0]
    return pl.pallas_call(
        _memkv_kernel,
        grid=(depth, B),
        in_specs=[
            pl.BlockSpec((1, M, D), lambda l, b: (b, 0, 0)),
            pl.BlockSpec((1, 1, D), lambda l, b: (l, 0, 0)),
            pl.BlockSpec((1, D, D), lambda l, b: (l, 0, 0)),
            pl.BlockSpec((1, D, D), lambda l, b: (l, 0, 0)),
        ],
        out_specs=[
            pl.BlockSpec((1, 1, M, D), lambda l, b: (l, b, 0, 0)),
            pl.BlockSpec((1, 1, M, D), lambda l, b: (l, b, 0, 0)),
        ],
        out_shape=[jax.ShapeDtypeStruct((depth, B, M, D), BF16)] * 2,
        compiler_params=_params(("arbitrary", "arbitrary")),
        name="memkv",
    )(mem, ln_memkv.reshape(depth, 1, D), w_mk, w_mv)


def _proj_kernel(has_y, tiles_per_seq, *refs):
    if has_y:
        x_ref, y_ref = refs[:2]
        refs = refs[2:]
        x = x_ref[...] + y_ref[...].astype(F32)
    else:
        x_ref = refs[0]
        refs = refs[1:]
        x = x_ref[...]
    (g_ref, wml_ref, wsqt_ref, wsk_ref, wsvt_ref, wg_ref, bg_ref, cw_ref, cb_ref,
     qka_ref, vo_ref, sqt_ref, sk_ref, svt_ref, gc_ref, gt_ref, qk_ref) = refs
    tm, W, L, P = x.shape[0], ML_WIDTH, ML_CHUNK, SB_SPAN

    @pl.when(pl.program_id(0) % tiles_per_seq == 0)
    def _():
        qk_ref[:8, :] = jnp.zeros((8, 2 * W), F32)

    h = _rms(x, g_ref[0]).astype(BF16)
    cw = cw_ref[0]
    cb = cb_ref[0]
    cblk = 2 * LANES

    def conv_block(c):
        rows = L
        for r in range(0, tm, rows):
            acc = cb[:, c:c + cblk]
            for tap in range(CONV_W):
                lo = 8 - (CONV_W - 1) + tap + r
                acc = acc + qk_ref[lo:lo + rows, c:c + cblk] * cw[tap:tap + 1, c:c + cblk]
            qa = acc * _sigmoid(acc)
            if c >= W:
                qa = qa * (ML_DHEAD ** -0.5)
            qka_ref[r:r + rows, c:c + cblk] = qa.astype(BF16)
        qk_ref[:8, c:c + cblk] = qk_ref[tm:, c:c + cblk]

    def store_v():
        vo_ref[:, :W] = _dot(h, wml_ref[0, :, 2 * W:3 * W]).astype(BF16)

    def store_o():
        vo_ref[:, W:] = _dot(h, wml_ref[0, :, 3 * W:]).astype(BF16)

    def store_sk():
        sk_ref[...] = _dot(h, wsk_ref[0]).astype(BF16)

    def store_sqt():
        sqt = (_dot_nt(wsqt_ref[0], h) * (SB_DHEAD ** -0.5 * LOG2E)).astype(BF16)
        for c in range(tm // P):
            sqt_ref[c] = sqt[:, c * P:(c + 1) * P]

    def store_svt():
        svt = _dot_nt(wsvt_ref[0], h).astype(BF16)
        for c in range(tm // P):
            svt_ref[c] = svt[:, c * P:(c + 1) * P]

    others = [store_v, store_o, store_sk, store_sqt, store_svt]
    blocks = list(range(0, 2 * W, cblk))
    for n, c in enumerate(blocks):
        qk_ref[8:, c:c + cblk] = _dot(h, wml_ref[0, :, c:c + cblk])
        if n > 0:
            conv_block(blocks[n - 1])
        if others:
            others.pop(0)()
    conv_block(blocks[-1])
    for rest in others:
        rest()
    gc = _dot(h, wg_ref[0]) + bg_ref[0]
    gc_ref[...] = gc
    gt = gc.T
    for c in range(tm // L):
        gt_ref[c] = gt[:8, c * L:(c + 1) * L]


def _proj(l, S, x2, y, ln_mix, w_ml, w_sqt, w_sk, w_svt, w_g, b_g, conv_w, conv_b):
    N, D = x2.shape
    tm, W, L, P, SW = TOKEN_TILE, ML_WIDTH, ML_CHUNK, SB_SPAN, SB_WIDTH
    row = lambda i: (i, 0)
    lay = lambda i: (l, 0, 0)
    ins = [x2] + ([y] if y is not None else [])
    in_specs = [pl.BlockSpec((tm, D), row)] * len(ins) + [
        pl.BlockSpec((1, 1, D), lay),
        pl.BlockSpec((1, D, 4 * W), lay),
        pl.BlockSpec((1, SW, D), lay),
        pl.BlockSpec((1, D, SW), lay),
        pl.BlockSpec((1, SW, D), lay),
        pl.BlockSpec((1, D, LANES), lay),
        pl.BlockSpec((1, 1, LANES), lay),
        pl.BlockSpec((1, CONV_W, 2 * W), lay),
        pl.BlockSpec((1, 1, 2 * W), lay),
    ]
    return pl.pallas_call(
        functools.partial(_proj_kernel, y is not None, S // tm),
        grid=(N // tm,),
        in_specs=in_specs,
        out_specs=[
            pl.BlockSpec((tm, 2 * W), row),
            pl.BlockSpec((tm, 2 * W), row),
            pl.BlockSpec((tm // P, SW, P), lambda i: (i, 0, 0)),
            pl.BlockSpec((tm, SW), row),
            pl.BlockSpec((tm // P, SW, P), lambda i: (i, 0, 0)),
            pl.BlockSpec((tm, LANES), row),
            pl.BlockSpec((tm // L, 8, L), lambda i: (i, 0, 0)),
        ],
        out_shape=[
            jax.ShapeDtypeStruct((N, 2 * W), BF16),
            jax.ShapeDtypeStruct((N, 2 * W), BF16),
            jax.ShapeDtypeStruct((N // P, SW, P), BF16),
            jax.ShapeDtypeStruct((N, SW), BF16),
            jax.ShapeDtypeStruct((N // P, SW, P), BF16),
            jax.ShapeDtypeStruct((N, LANES), F32),
            jax.ShapeDtypeStruct((N // L, 8, L), F32),
        ],
        scratch_shapes=[pltpu.VMEM((8 + tm, 2 * W), F32)],
        compiler_params=_params(("arbitrary",)),
        name="proj",
    )(*ins, ln_mix, w_ml, w_sqt, w_sk, w_svt, w_g, b_g, conv_w, conv_b)


def _mlstm_kernel(qk_ref, vo_ref, gc_ref, gt_ref, hg_ref, out_ref, s_ref, m_ref):
    L, d, W, H = ML_CHUNK, ML_DHEAD, ML_WIDTH, ML_HEADS
    nb = qk_ref.shape[0]

    @pl.when(pl.program_id(1) == 0)
    def _():
        s_ref[...] = jnp.zeros_like(s_ref)
        m_ref[...] = jnp.zeros_like(m_ref)

    row = lax.broadcasted_iota(I32, (L, L), 0)
    col = lax.broadcasted_iota(I32, (L, L), 1)
    causal = col <= row
    tri = jnp.where(causal, 1.0, 0.0).astype(BF16)
    tri_t = jnp.where(row <= col, 1.0, 0.0).astype(BF16)
    lane = lax.broadcasted_iota(I32, (L, LANES), 1)
    ones_cols = jnp.ones((L, d), BF16)
    ones_sq = jnp.ones((d, d), BF16)
    r2 = lax.broadcasted_iota(I32, (LANES, 2 * d), 0)
    c2 = lax.broadcasted_iota(I32, (LANES, 2 * d), 1)
    pick = [jnp.where(((c2 < d) & (r2 == H + h)) | ((c2 >= d) & (r2 == h)), 1.0, 0.0).astype(BF16)
            for h in range(H)]
    hg = hg_ref[0]
    units = [(bb, h) for bb in range(nb) for h in range(H)]

    gts, bts, xs = [], [], []
    for bb in range(nb):
        gt = gt_ref[bb, 0]
        gc = gc_ref[bb]
        lsc_hi, lsc_lo = _split(-_softplus(-gc))
        bc = _dot(tri, lsc_hi) + _dot(tri, lsc_lo)
        lst_hi, lst_lo = _split(-_softplus(-gt))
        gts.append(gt)
        bts.append(_dot(lst_hi, tri_t) + _dot(lst_lo, tri_t))
        xs.append(_split(jnp.where(lane < H, gc, bc)))

    qbs, kbs, v1s, qks, carried, spread = [], [], [], [], [], []
    for bb, h in units:
        qb = qk_ref[bb, :, h * d:(h + 1) * d]
        kb = qk_ref[bb, :, W + h * d:W + (h + 1) * d]
        v = vo_ref[bb, :, h * d:(h + 1) * d]
        qbs.append(qb)
        kbs.append(kb)
        v1s.append(jnp.concatenate([v, ones_cols], axis=1))
        qks.append(_dot_nt(qb, kb))
        carried.append(_dot(qb, s_ref[bb * H + h].astype(BF16)))
        spread.append(_dot(xs[bb][0], pick[h]) + _dot(xs[bb][1], pick[h]))

    ss, m_ts, w_inters = [], [], []
    for u, (bb, h) in enumerate(units):
        b_c = spread[u][:, :d]
        i_r = gts[bb][h:h + 1, :]
        b_r = bts[bb][H + h:H + h + 1, :]
        logd = jnp.where(causal, b_c - b_r + i_r, -jnp.inf)
        inter = b_c + m_ref[u][:1, :]
        m_t = jnp.maximum(inter, jnp.max(logd, axis=-1, keepdims=True))
        ss.append((qks[u] * jnp.exp(logd - m_t)).astype(BF16))
        m_ts.append(m_t)
        w_inters.append(jnp.exp(inter - m_t))

    intras = [_dot(ss[u], v1s[u]) for u in range(len(units))]

    hhs = []
    for u in range(len(units)):
        w2 = jnp.concatenate([w_inters[u], w_inters[u]], axis=1)
        tot = intras[u] + w2 * carried[u]
        hhs.append(tot[:, :d] / jnp.maximum(jnp.abs(tot[:, d:]), jnp.exp(-m_ts[u])))
    sqs = [_dot((hh * hh).astype(BF16), ones_sq) * (1.0 / d) for hh in hhs]
    for u, (bb, h) in enumerate(units):
        og = vo_ref[bb, :, W + h * d:W + (h + 1) * d].astype(F32)
        hn = hhs[u] * lax.rsqrt(sqs[u] + EPS) * hg[:, h * d:(h + 1) * d]
        out_ref[bb, :, h * d:(h + 1) * d] = (_sigmoid(og) * hn).astype(BF16)

    kws, w_olds, m_news = [], [], []
    for u, (bb, h) in enumerate(units):
        b_c, i_c = spread[u][:, :d], spread[u][:, d:]
        i_r = gts[bb][h:h + 1, :]
        b_r = bts[bb][H + h:H + h + 1, :]
        m_prev = m_ref[u][:1, :1]
        b_last = b_r[:, L - 1:L]
        m_new = jnp.maximum(b_last + m_prev, jnp.max(b_last - b_r + i_r, axis=-1, keepdims=True))
        w_olds.append(jnp.exp(b_last + m_prev - m_new))
        m_news.append(m_new)
        kws.append((kbs[u].astype(F32) * jnp.exp(b_last - b_c + i_c - m_new)).T.astype(BF16))
    for u in range(len(units)):
        s_ref[u] = w_olds[u] * s_ref[u] + _dot(kws[u], v1s[u])
        m_ref[u] = jnp.broadcast_to(m_news[u], (8, LANES))


def _mlstm(l, B, S, qka, vo, gc, gt, ml_head_g):
    L, W, H, d = ML_CHUNK, ML_WIDTH, ML_HEADS, ML_DHEAD
    nc = S // L
    nb = 4 if B % 4 == 0 else (2 if B % 2 == 0 else 1)
    out = pl.pallas_call(
        _mlstm_kernel,
        grid=(B // nb, nc),
        in_specs=[
            pl.BlockSpec((nb, L, 2 * W), lambda b, c: (b, c, 0)),
            pl.BlockSpec((nb, L, 2 * W), lambda b, c: (b, c, 0)),
            pl.BlockSpec((nb, L, LANES), lambda b, c: (b, c, 0)),
            pl.BlockSpec((nb, 1, 8, L), lambda b, c: (b, c, 0, 0)),
            pl.BlockSpec((1, 1, W), lambda b, c: (l, 0, 0)),
        ],
        out_specs=pl.BlockSpec((nb, L, W), lambda b, c: (b, c, 0)),
        out_shape=jax.ShapeDtypeStruct((B, S, W), BF16),
        scratch_shapes=[
            pltpu.VMEM((nb * H, d, 2 * d), F32),
            pltpu.VMEM((nb * H, 8, LANES), F32),
        ],
        compiler_params=_params(("arbitrary", "arbitrary")),
        name="mlstm",
    )(qka.reshape(B, S, 2 * W), vo.reshape(B, S, 2 * W), gc.reshape(B, S, LANES), gt.reshape(B, nc, 8, L),
      ml_head_g)
    return out.reshape(B * S, W)


def _sb_kernel(qt_ref, k_ref, vt_ref, g_ref, o_ref, acc_ref):
    P, H, d = SB_SPAN, SB_HEADS, SB_DHEAD
    i = pl.program_id(1)
    row = lax.broadcasted_iota(I32, (P, P), 0)
    col = lax.broadcasted_iota(I32, (P, P), 1)
    strict = row < col
    neg_from = jnp.where(col >= row, -1.0, 0.0).astype(BF16)

    heads = [slice(h * d, (h + 1) * d) for h in range(H)]

    def spans(js, csums, diagonal):
        units = [(n, j, h) for n, j in enumerate(js) for h in range(H)]
        masked = [diagonal and n == 0 for n, _, _ in units]
        ks = [k_ref[pl.ds(pl.multiple_of(j * P, P), P), heads[h]] for _, j, h in units]
        qts = [qt_ref[0, heads[h], :] for _, _, h in units]
        zs = [_dot(kj, qt) for kj, qt in zip(ks, qts)]
        sps = []
        for z, m in zip(zs, masked):
            sp = jnp.where(z > SP_LINEAR, z, jnp.log(1.0 + jnp.exp2(z)) * LOG2E)
            sps.append(jnp.where(strict, sp, 0.0) if m else sp)
        tails = [_dot(neg_from, sp.astype(BF16)) for sp in sps]
        csums = list(csums) if csums is not None else [None] * H
        parts = [None] * H
        for u, (_, j, h) in enumerate(units):
            e = zs[u] + tails[u]
            total = tails[u][0:1, :]
            if masked[u]:
                a = jnp.where(strict, jnp.exp2(e), 0.0)
                csums[h] = -total
            else:
                a = jnp.exp2(e - csums[h])
                csums[h] = csums[h] - total
            part = _dot(vt_ref[j, heads[h], :], a.astype(BF16))
            parts[h] = part if parts[h] is None else parts[h] + part
        for h in range(H):
            if diagonal:
                acc_ref[heads[h], :] = parts[h]
            else:
                acc_ref[heads[h], :] += parts[h]
        return tuple(csums)

    @pl.when(i == 0)
    def _():
        spans([i], None, True)

    @pl.when(i > 0)
    def _():
        csums = spans([i, i - 1], None, True)
        left = i - 1
        odd = left % 2
        csums = lax.fori_loop(0, odd, lambda jj, c: spans([i - 2], c, False), csums)
        lax.fori_loop(0, left // 2, lambda p, c: spans([i - 2 - odd - 2 * p, i - 3 - odd - 2 * p], c, False), csums)
    g = g_ref[0]
    for h in range(H):
        hs = slice(h * d, (h + 1) * d)
        acc = acc_ref[hs, :]
        on = acc * lax.rsqrt(jnp.mean(acc * acc, axis=0, keepdims=True) + EPS)
        o_ref[:, hs] = (on.T * g[:, hs]).astype(BF16)


def _stickbreak(l, B, S, sqt, sk, svt, sb_head_g):
    P, W = SB_SPAN, SB_WIDTH
    nq = S // P
    return pl.pallas_call(
        _sb_kernel,
        grid=(B, nq),
        in_specs=[
            pl.BlockSpec((1, W, P), lambda b, i: (b * nq + i, 0, 0)),
            pl.BlockSpec((S, W), lambda b, i: (b, 0)),
            pl.BlockSpec((nq, W, P), lambda b, i: (b, 0, 0)),
            pl.BlockSpec((1, 1, W), lambda b, i: (l, 0, 0)),
        ],
        out_specs=pl.BlockSpec((P, W), lambda b, i: (b * nq + i, 0)),
        out_shape=jax.ShapeDtypeStruct((B * S, W), BF16),
        scratch_shapes=[pltpu.VMEM((W, P), F32)],
        compiler_params=_params(("arbitrary", "arbitrary")),
        name="stickbreak",
    )(sqt, sk, svt, sb_head_g)


def _route(lgt, tm):
    G, E = N_GROUPS, EXP_PER_GROUP
    lg = [lgt[i:i + 1, :] for i in range(G + G * E)]
    mg = functools.reduce(jnp.maximum, lg[:G])
    eg = [jnp.exp(v - mg) for v in lg[:G]]
    zg = functools.reduce(jnp.add, eg)
    pg = [v / zg for v in eg]
    p_sel = functools.reduce(jnp.maximum, pg)
    gid = jnp.where(pg[0] == p_sel, 0, jnp.where(pg[1] == p_sel, 1, jnp.where(pg[2] == p_sel, 2, 3)))
    es = [jnp.where(gid == 0, lg[G + e], jnp.where(gid == 1, lg[G + E + e],
          jnp.where(gid == 2, lg[G + 2 * E + e], lg[G + 3 * E + e]))) for e in range(E)]
    me = functools.reduce(jnp.maximum, es)
    ee = [jnp.exp(v - me) for v in es]
    ze = functools.reduce(jnp.add, ee)
    pe = [v / ze for v in ee]

    def first_max(vals):
        top = functools.reduce(jnp.maximum, vals)
        idx = jnp.where(vals[0] == top, 0, jnp.where(vals[1] == top, 1, jnp.where(vals[2] == top, 2, 3)))
        return top, idx

    v1, i1 = first_max(pe)
    v2, i2 = first_max([jnp.where(i1 == e, -1.0, pe[e]) for e in range(E)])
    tsum = v1 + v2
    tw1, tw2 = v1 / tsum, v2 / tsum
    gates = [p_sel * (jnp.where(i1 == e, tw1, 0.0) + jnp.where(i2 == e, tw2, 0.0)) for e in range(E)]

    sub8 = lax.broadcasted_iota(I32, (8, tm), 0)
    onehot = jnp.where(sub8 == gid, 1.0, 0.0)
    r = lax.broadcasted_iota(I32, (SORT_BLOCK, SORT_BLOCK), 0)
    c = lax.broadcasted_iota(I32, (SORT_BLOCK, SORT_BLOCK), 1)
    before = jnp.where(r < c, 1.0, 0.0).astype(BF16)
    lane = lax.broadcasted_iota(I32, (8, LANES), 1)
    counts = jnp.zeros((8, LANES), F32)
    lps = []
    for jj in range(tm // SORT_BLOCK):
        oj = onehot[:, jj * SORT_BLOCK:(jj + 1) * SORT_BLOCK]
        pre = _dot(oj.astype(BF16), before)
        cnt = jnp.sum(oj, axis=-1, keepdims=True)
        pcnt = jnp.floor((cnt + (ROW_TILE_BF16 - 1.0)) * (1.0 / ROW_TILE_BF16)) * ROW_TILE_BF16
        start = jnp.zeros((1, 1), F32)
        lp = jnp.zeros((1, SORT_BLOCK), F32)
        for g in range(G):
            lp = lp + oj[g:g + 1, :] * (pre[g:g + 1, :] + start)
            start = start + pcnt[g:g + 1, :]
        lps.append(lp)
        counts = counts + jnp.where(lane == jj, cnt, 0.0)
    lp = jnp.concatenate(lps, axis=1)
    rows = gates + [gid.astype(F32), lp]
    pack = jnp.zeros((8, tm), F32)
    for kk, v in enumerate(rows):
        pack = pack + jnp.where(sub8 == kk, v, 0.0)
    return pack, counts.astype(I32)


def _post_kernel(has_y, *refs):
    if has_y:
        x_ref, y_ref = refs[:2]
        refs = refs[2:]
        x = x_ref[...] + y_ref[...].astype(F32)
    else:
        x_ref = refs[0]
        refs = refs[1:]
        x = x_ref[...]
    (hml_ref, hsb_ref, wo_ref, gmem_ref, wq_ref, k_ref, v_ref, wmo_ref, gffn_ref, wrh_ref, wrl_ref, br_ref,
     x2_ref, h3_ref, rt_ref, gcol_ref, cnt_ref) = refs
    tm = x.shape[0]
    x1 = x + _dot(hml_ref[...], wo_ref[0, :ML_WIDTH, :]) + _dot(hsb_ref[...], wo_ref[0, ML_WIDTH:, :])
    q = _dot(_rms(x1, gmem_ref[0]).astype(BF16), wq_ref[0]).astype(BF16)
    kk = k_ref[0, 0]
    vv = v_ref[0, 0]
    dh = q.shape[1] // XA_HEADS
    heads = [slice(hd * dh, (hd + 1) * dh) for hd in range(XA_HEADS)]
    scores = [_dot_nt(q[:, sl], kk[:, sl]) * (dh ** -0.5) for sl in heads]
    probs = []
    for s in scores:
        p = jnp.exp(s - jnp.max(s, axis=-1, keepdims=True))
        probs.append((p / jnp.sum(p, axis=-1, keepdims=True)).astype(BF16))
    outs = [_dot(p, vv[:, sl]).astype(BF16) for p, sl in zip(probs, heads)]
    x2 = x1 + _dot(jnp.concatenate(outs, axis=-1), wmo_ref[0])
    x2_ref[...] = x2
    h3_hi, h3_lo = _split(_rms(x2, gffn_ref[0]))
    h3_ref[...] = h3_hi
    wrh = wrh_ref[0]
    both = _dot(h3_hi, jnp.concatenate([wrh, wrl_ref[0]], axis=1))
    lg = both[:, :LANES] + both[:, LANES:] + _dot(h3_lo, wrh) + br_ref[0]
    pack, counts = _route(lg.T, tm)
    rt_ref[...] = pack
    gcol_ref[...] = jnp.concatenate([pack, jnp.zeros((LANES - 8, tm), F32)], axis=0).T
    cnt_ref[0] = counts


def _post(l, B, S, x2, y, hml, hsb, w_out, ln_mem, w_mq, kx, vx, w_mo, ln_ffn, wr_hi, wr_lo, b_r):
    N, D = x2.shape
    M = kx.shape[2]
    tm = TOKEN_TILE
    tpb = S // tm
    row = lambda i: (i, 0)
    lay = lambda i: (l, 0, 0)
    ins = [x2] + ([y] if y is not None else [])
    in_specs = [pl.BlockSpec((tm, D), row)] * len(ins) + [
        pl.BlockSpec((tm, ML_WIDTH), row),
        pl.BlockSpec((tm, SB_WIDTH), row),
        pl.BlockSpec((1, D, D), lay),
        pl.BlockSpec((1, 1, D), lay),
        pl.BlockSpec((1, D, D), lay),
        pl.BlockSpec((1, 1, M, D), lambda i: (l, i // tpb, 0, 0)),
        pl.BlockSpec((1, 1, M, D), lambda i: (l, i // tpb, 0, 0)),
        pl.BlockSpec((1, D, D), lay),
        pl.BlockSpec((1, 1, D), lay),
        pl.BlockSpec((1, D, LANES), lay),
        pl.BlockSpec((1, D, LANES), lay),
        pl.BlockSpec((1, 1, LANES), lay),
    ]
    return pl.pallas_call(
        functools.partial(_post_kernel, y is not None),
        grid=(N // tm,),
        in_specs=in_specs,
        out_specs=[
            pl.BlockSpec((tm, D), row),
            pl.BlockSpec((tm, D), row),
            pl.BlockSpec((8, tm), lambda i: (0, i)),
            pl.BlockSpec((tm, LANES), row),
            pl.BlockSpec((1, 8, LANES), lambda i: (i, 0, 0)),
        ],
        out_shape=[
            jax.ShapeDtypeStruct((N, D), F32),
            jax.ShapeDtypeStruct((N, D), BF16),
            jax.ShapeDtypeStruct((8, N), F32),
            jax.ShapeDtypeStruct((N, LANES), F32),
            jax.ShapeDtypeStruct((N // tm, 8, LANES), I32),
        ],
        compiler_params=_params(("arbitrary",)),
        name="post",
    )(*ins, hml, hsb, w_out, ln_mem, w_mq, kx, vx, w_mo, ln_ffn, wr_hi, wr_lo, b_r)


def _moe_kernel(T, cnt_ref, h3_ref, rt_ref, gcol_ref, w1_ref, w3_ref, w2_ref, y_ref,
                xs_ref, gs_ref, ys_ref, sl_ref, gl_ref):
    G, CH, CH_TAIL, SBK, WIN, RT = N_GROUPS, MOE_CHUNK, MOE_CHUNK_TAIL, SORT_BLOCK, SORT_WIN, ROW_TILE_BF16
    b, g, hf = pl.program_id(0), pl.program_id(1), pl.program_id(2)
    nsb = T // SBK
    nbat = sl_ref.shape[0]
    n_half = pl.num_programs(2)
    D = h3_ref.shape[1]

    pc = [[((cnt_ref[(b * nsb + j) * G + gg] + (RT - 1)) // RT) * RT for j in range(nsb)] for gg in range(G)]
    base = [0]
    for gg in range(G):
        base.append(base[-1] + functools.reduce(lambda u, w: u + w, pc[gg]))

    def seg_starts(j):
        out, src = [], 0
        for gg in range(G):
            dst = base[gg]
            for jp in range(j):
                dst = dst + pc[gg][jp]
            out.append((src, dst))
            src = src + pc[gg][j]
        return out

    @pl.when((g == 0) & (hf == 0))
    def _sort():
        total = pl.multiple_of(base[G], RT)
        xs_ref[pl.ds(total, CH), :] = jnp.zeros((CH, D), BF16)
        gs_ref[pl.ds(total, CH), :] = jnp.zeros((CH, LANES), F32)
        riota = lax.broadcasted_iota(I32, (WIN, SBK), 0)
        for j0 in range(0, nsb, nbat):
            for jj in range(nbat):
                j = j0 + jj
                lp = rt_ref[5:6, j * SBK:(j + 1) * SBK].astype(I32)
                perm = jnp.where(riota == lp, 1.0, 0.0).astype(BF16)
                sl_ref[jj] = _dot(perm, h3_ref[j * SBK:(j + 1) * SBK, :]).astype(BF16)
                g_hi, g_lo = _split(gcol_ref[j * SBK:(j + 1) * SBK, :])
                both = _dot(perm, jnp.concatenate([g_hi, g_lo], axis=1))
                gl_ref[jj] = both[:, :LANES] + both[:, LANES:]
            for jj in range(nbat):
                for gg, (src, dst) in enumerate(seg_starts(j0 + jj)):
                    def copy_in(c, carry, jj=jj, src=src, dst=dst):
                        s = pl.multiple_of(src + c * RT, RT)
                        t = pl.multiple_of(dst + c * RT, RT)
                        xs_ref[pl.ds(t, RT), :] = sl_ref[jj, pl.ds(s, RT), :]
                        gs_ref[pl.ds(t, RT), :] = gl_ref[jj, pl.ds(s, RT), :]
                        return carry
                    lax.fori_loop(0, pc[gg][j0 + jj] // RT, copy_in, 0)

    rows_g = base[1] - base[0]
    base_g = base[0]
    for gg in range(1, G):
        rows_g = jnp.where(g == gg, base[gg + 1] - base[gg], rows_g)
        base_g = jnp.where(g == gg, base[gg], base_g)

    def run_chunk(r0, rows):
        r0 = pl.multiple_of(r0, RT)
        xc = xs_ref[pl.ds(r0, rows), :]
        gc = gs_ref[pl.ds(r0, rows), :]
        ups = [(_dot(xc, w1_ref[e]), _dot(xc, w3_ref[e])) for e in range(EXPERTS_PER_STEP)]
        hms = []
        for e, (a, u) in enumerate(ups):
            gate = gc[:, e:e + 1]
            for k in range(1, EXP_PER_GROUP // EXPERTS_PER_STEP):
                gate = jnp.where(hf == k, gc[:, k * EXPERTS_PER_STEP + e:k * EXPERTS_PER_STEP + e + 1], gate)
            hms.append(((a * _sigmoid(a)) * u * gate).astype(BF16))
        acc = functools.reduce(jnp.add, [_dot(hm, w2_ref[e]) for e, hm in enumerate(hms)])

        @pl.when(hf == 0)
        def _():
            ys_ref[pl.ds(r0, rows), :] = acc

        @pl.when(hf != 0)
        def _():
            ys_ref[pl.ds(r0, rows), :] += acc

    start, left = base_g, rows_g
    for size in (2 * CH, CH):
        n = left // size

        def body(c, carry, start=start, size=size):
            run_chunk(start + c * size, size)
            return carry

        lax.fori_loop(0, n, body, 0)
        start, left = start + n * size, left - n * size

    def tail_body(c, carry, start=start):
        run_chunk(start + c * CH_TAIL, CH_TAIL)
        return carry

    lax.fori_loop(0, (left + (CH_TAIL - 1)) // CH_TAIL, tail_body, 0)

    @pl.when((g == G - 1) & (hf == n_half - 1))
    def _unsort():
        ciota = lax.broadcasted_iota(I32, (SBK, WIN), 1)
        for j0 in range(0, nsb, nbat):
            for jj in range(nbat):
                for gg, (src, dst) in enumerate(seg_starts(j0 + jj)):
                    def copy_out(c, carry, jj=jj, src=src, dst=dst):
                        s = pl.multiple_of(src + c * RT, RT)
                        t = pl.multiple_of(dst + c * RT, RT)
                        sl_ref[jj, pl.ds(s, RT), :] = ys_ref[pl.ds(t, RT), :].astype(BF16)
                        return carry
                    lax.fori_loop(0, pc[gg][j0 + jj] // RT, copy_out, 0)
            for jj in range(nbat):
                j = j0 + jj
                lp = gcol_ref[j * SBK:(j + 1) * SBK, 5:6].astype(I32)
                perm_t = jnp.where(ciota == lp, 1.0, 0.0).astype(BF16)
                y_ref[j * SBK:(j + 1) * SBK, :] = _dot(perm_t, sl_ref[jj]).astype(BF16)


def _moe(l, T, counts, h3, rt, gcol, w1, w3, w2):
    N, D = h3.shape
    G, E, EPS_ = N_GROUPS, EXP_PER_GROUP, EXPERTS_PER_STEP
    F = w1.shape[-1]
    rows = T + (T // SORT_BLOCK) * G * (ROW_TILE_BF16 - 1) + MOE_CHUNK
    rows = -(-rows // ROW_TILE_BF16) * ROW_TILE_BF16
    nsb = T // SORT_BLOCK
    nbat = next(n for n in (SORT_BATCH, 2, 1) if nsb % n == 0)
    grid_spec = pltpu.PrefetchScalarGridSpec(
        num_scalar_prefetch=1,
        grid=(N // T, G, E // EPS_),
        in_specs=[
            pl.BlockSpec((T, D), lambda b, g, hf, cnt: (b, 0)),
            pl.BlockSpec((8, T), lambda b, g, hf, cnt: (0, b)),
            pl.BlockSpec((T, LANES), lambda b, g, hf, cnt: (b, 0)),
            pl.BlockSpec((None, None, EPS_, D, F), lambda b, g, hf, cnt: (l, g, hf, 0, 0)),
            pl.BlockSpec((None, None, EPS_, D, F), lambda b, g, hf, cnt: (l, g, hf, 0, 0)),
            pl.BlockSpec((None, None, EPS_, F, D), lambda b, g, hf, cnt: (l, g, hf, 0, 0)),
        ],
        out_specs=pl.BlockSpec((T, D), lambda b, g, hf, cnt: (b, 0)),
        scratch_shapes=[
            pltpu.VMEM((rows, D), BF16),
            pltpu.VMEM((rows, LANES), F32),
            pltpu.VMEM((rows, D), F32),
            pltpu.VMEM((nbat, SORT_WIN, D), BF16),
            pltpu.VMEM((nbat, SORT_WIN, LANES), F32),
        ],
    )
    return pl.pallas_call(
        functools.partial(_moe_kernel, T),
        grid_spec=grid_spec,
        out_shape=jax.ShapeDtypeStruct((N, D), BF16),
        compiler_params=_params(("arbitrary", "arbitrary", "arbitrary")),
        name="moe",
    )(counts, h3, rt, gcol, w1, w3, w2)


def _final_kernel(x_ref, y_ref, g_ref, o_ref):
    o_ref[...] = _rms(x_ref[...] + y_ref[...].astype(F32), g_ref[...])


def _final(x2, y, ln_final):
    N, D = x2.shape
    tm = TOKEN_TILE
    return pl.pallas_call(
        _final_kernel,
        grid=(N // tm,),
        in_specs=[pl.BlockSpec((tm, D), lambda i: (i, 0)), pl.BlockSpec((tm, D), lambda i: (i, 0)),
                  pl.BlockSpec((1, D), lambda i: (0, 0))],
        out_specs=pl.BlockSpec((tm, D), lambda i: (i, 0)),
        out_shape=jax.ShapeDtypeStruct((N, D), F32),
        compiler_params=_params(("arbitrary",)),
        name="final_norm",
    )(x2, y, ln_final.reshape(1, D))


def kernel(x, mem, ln_mix, w_in, conv_w, conv_b, i_bias, f_bias, ml_head_g, sb_head_g, w_out, ln_mem, ln_memkv,
           w_mq, w_mk, w_mv, w_mo, ln_ffn, w_rg, b_rg, w_re, b_re, w_e1, w_e3, w_e2, ln_final):
    B, S, D = x.shape
    N = B * S
    depth = w_in.shape[0]
    G, E = N_GROUPS, EXP_PER_GROUP
    assert S % TOKEN_TILE == 0 and TOKEN_TILE % ML_CHUNK == 0 and TOKEN_TILE % SORT_BLOCK == 0 and S % SB_SPAN == 0
    T = 2048 if S % 2048 == 0 else S

    c0, c1 = 4 * ML_WIDTH, 4 * ML_WIDTH + 2 * ML_HEADS
    w_ml = w_in[:, :, :c0].astype(BF16)
    w_sqt = jnp.swapaxes(w_in[:, :, c1:c1 + SB_WIDTH], 1, 2).astype(BF16)
    w_sk = w_in[:, :, c1 + SB_WIDTH:c1 + 2 * SB_WIDTH].astype(BF16)
    w_svt = jnp.swapaxes(w_in[:, :, c1 + 2 * SB_WIDTH:], 1, 2).astype(BF16)
    w_gate = w_in[:, :, c0:c1]
    w_g = jnp.pad(w_gate, ((0, 0), (0, 0), (0, LANES - 8))).astype(BF16)
    gate_b = jnp.concatenate([i_bias, f_bias], axis=-1)
    b_g = jnp.pad(gate_b, ((0, 0), (0, LANES - 8))).reshape(depth, 1, LANES)
    w_r = jnp.pad(jnp.concatenate([w_rg, w_re], axis=-1), ((0, 0), (0, 0), (0, LANES - G - G * E)))
    wr_hi = w_r.astype(BF16)
    wr_lo = (w_r - wr_hi.astype(F32)).astype(BF16)
    b_r = jnp.pad(jnp.concatenate([b_rg, b_re], axis=-1), ((0, 0), (0, LANES - G - G * E))).reshape(depth, 1, LANES)
    w_out_b, w_mq_b, w_mo_b = w_out.astype(BF16), w_mq.astype(BF16), w_mo.astype(BF16)
    w1, w3, w2 = w_e1.astype(BF16), w_e3.astype(BF16), w_e2.astype(BF16)
    r3 = lambda a: a.reshape(depth, 1, a.shape[-1])

    kx, vx = _memkv(mem, ln_memkv, w_mk.astype(BF16), w_mv.astype(BF16))

    x2, y = x.reshape(N, D), None
    nsub = TOKEN_TILE // SORT_BLOCK
    for l in range(depth):
        qka, vo, sqt, sk, svt, gc, gt = _proj(l, S, x2, y, r3(ln_mix), w_ml, w_sqt, w_sk, w_svt, w_g, b_g,
                                              conv_w, r3(conv_b))
        hml = _mlstm(l, B, S, qka, vo, gc, gt, r3(ml_head_g))
        hsb = _stickbreak(l, B, S, sqt, sk, svt, r3(sb_head_g))
        x2, h3, rt, gcol, cnt = _post(l, B, S, x2, y, hml, hsb, w_out_b, r3(ln_mem), w_mq_b, kx, vx, w_mo_b,
                                      r3(ln_ffn), wr_hi, wr_lo, b_r)
        counts = jnp.swapaxes(cnt[:, :G, :nsub], 1, 2).reshape(-1)
        y = _moe(l, T, counts, h3, rt, gcol, w1, w3, w2)
    return _final(x2, y, ln_final).reshape(B, S, D)
```

```python
import functools

import jax
import jax.numpy as jnp
from jax import lax
from jax.experimental import pallas as pl
from jax.experimental.pallas import tpu as pltpu

F32 = jnp.float32
BF16 = jnp.bfloat16
I32 = jnp.int32

ML_HEADS = 4
ML_DHEAD = 128
SB_HEADS = 4
SB_DHEAD = 128
ML_WIDTH = ML_HEADS * ML_DHEAD
SB_WIDTH = SB_HEADS * SB_DHEAD
CONV_W = 4
ML_CHUNK = 128
SB_SPAN = 256
XA_HEADS = 4
N_GROUPS = 4
EXP_PER_GROUP = 4
EPS = 1e-6
LOG2E = 1.4426950408889634
SP_LINEAR = 100.0

LANES = 128
ROW_TILE_BF16 = 16
SORT_BLOCK = 256
SORT_WIN = 384
SORT_BATCH = 4
MOE_CHUNK = 256
MOE_CHUNK_TAIL = 128
EXPERTS_PER_STEP = 2
TOKEN_TILE = 1024
VMEM_LIMIT = 56 * 1024 * 1024


def _dot(a, b):
    return jnp.dot(a, b, preferred_element_type=F32)


def _dot_nt(a, b):
    return lax.dot_general(a, b, (((1,), (1,)), ((), ())), preferred_element_type=F32)


def _split(x):
    hi = x.astype(BF16)
    lo = (x - hi.astype(F32)).astype(BF16)
    return hi, lo


def _rms(x, g):
    return x * lax.rsqrt(jnp.mean(x * x, axis=-1, keepdims=True) + EPS) * g


def _sigmoid(x):
    return 0.5 * jnp.tanh(0.5 * x) + 0.5


def _softplus(x):
    return jnp.maximum(x, 0.0) + jnp.log1p(jnp.exp(-jnp.abs(x)))


def _params(sem):
    return pltpu.CompilerParams(dimension_semantics=sem, vmem_limit_bytes=VMEM_LIMIT)


def _memkv_kernel(mem_ref, g_ref, wk_ref, wv_ref, k_ref, v_ref):
    h = _rms(mem_ref[0], g_ref[0]).astype(BF16)
    k_ref[0, 0] = _dot(h, wk_ref[0]).astype(BF16)
    v_ref[0, 0] = _dot(h, wv_ref[0]).astype(BF16)


def _memkv(mem, ln_memkv, w_mk, w_mv):
    B, M, D = mem.shape
    depth = w_mk.shape[0]
    return pl.pallas_call(
        _memkv_kernel,
        grid=(depth, B),
        in_specs=[
            pl.BlockSpec((1, M, D), lambda l, b: (b, 0, 0)),
            pl.BlockSpec((1, 1, D), lambda l, b: (l, 0, 0)),
            pl.BlockSpec((1, D, D), lambda l, b: (l, 0, 0)),
            pl.BlockSpec((1, D, D), lambda l, b: (l, 0, 0)),
        ],
        out_specs=[
            pl.BlockSpec((1, 1, M, D), lambda l, b: (l, b, 0, 0)),
            pl.BlockSpec((1, 1, M, D), lambda l, b: (l, b, 0, 0)),
        ],
        out_shape=[jax.ShapeDtypeStruct((depth, B, M, D), BF16)] * 2,
        compiler_params=_params(("arbitrary", "arbitrary")),
        name="memkv",
    )(mem, ln_memkv.reshape(depth, 1, D), w_mk, w_mv)


def _proj_kernel(has_y, tiles_per_seq, *refs):
    if has_y:
        x_ref, y_ref = refs[:2]
        refs = refs[2:]
        x = x_ref[...] + y_ref[...].astype(F32)
    else:
        x_ref = refs[0]
        refs = refs[1:]
        x = x_ref[...]
    (g_ref, wml_ref, wsqt_ref, wsk_ref, wsvt_ref, wg_ref, bg_ref, cw_ref, cb_ref,
     qka_ref, vo_ref, sqt_ref, sk_ref, svt_ref, gc_ref, gt_ref, qk_ref) = refs
    tm, W, L, P = x.shape[0], ML_WIDTH, ML_CHUNK, SB_SPAN

    @pl.when(pl.program_id(0) % tiles_per_seq == 0)
    def _():
        qk_ref[:8, :] = jnp.zeros((8, 2 * W), F32)

    h = _rms(x, g_ref[0]).astype(BF16)
    cw = cw_ref[0]
    cb = cb_ref[0]
    cblk = 2 * LANES

    def conv_block(c):
        rows = L
        for r in range(0, tm, rows):
            acc = cb[:, c:c + cblk]
            for tap in range(CONV_W):
                lo = 8 - (CONV_W - 1) + tap + r
                acc = acc + qk_ref[lo:lo + rows, c:c + cblk] * cw[tap:tap + 1, c:c + cblk]
            qa = acc * _sigmoid(acc)
            if c >= W:
                qa = qa * (ML_DHEAD ** -0.5)
            qka_ref[r:r + rows, c:c + cblk] = qa.astype(BF16)
        qk_ref[:8, c:c + cblk] = qk_ref[tm:, c:c + cblk]

    def store_v():
        vo_ref[:, :W] = _dot(h, wml_ref[0, :, 2 * W:3 * W]).astype(BF16)

    def store_o():
        vo_ref[:, W:] = _dot(h, wml_ref[0, :, 3 * W:]).astype(BF16)

    def store_sk():
        sk_ref[...] = _dot(h, wsk_ref[0]).astype(BF16)

    def store_sqt():
        sqt = (_dot_nt(wsqt_ref[0], h) * (SB_DHEAD ** -0.5 * LOG2E)).astype(BF16)
        for c in range(tm // P):
            sqt_ref[c] = sqt[:, c * P:(c + 1) * P]

    def store_svt():
        svt = _dot_nt(wsvt_ref[0], h).astype(BF16)
        for c in range(tm // P):
            svt_ref[c] = svt[:, c * P:(c + 1) * P]

    others = [store_v, store_o, store_sk, store_sqt, store_svt]
    blocks = list(range(0, 2 * W, cblk))
    for n, c in enumerate(blocks):
        qk_ref[8:, c:c + cblk] = _dot(h, wml_ref[0, :, c:c + cblk])
        if n > 0:
            conv_block(blocks[n - 1])
        if others:
            others.pop(0)()
    conv_block(blocks[-1])
    for rest in others:
        rest()
    gc = _dot(h, wg_ref[0]) + bg_ref[0]
    gc_ref[...] = gc
    gt = gc.T
    for c in range(tm // L):
        gt_ref[c] = gt[:8, c * L:(c + 1) * L]


def _proj(l, S, x2, y, ln_mix, w_ml, w_sqt, w_sk, w_svt, w_g, b_g, conv_w, conv_b):
    N, D = x2.shape
    tm, W, L, P, SW = TOKEN_TILE, ML_WIDTH, ML_CHUNK, SB_SPAN, SB_WIDTH
    row = lambda i: (i, 0)
    lay = lambda i: (l, 0, 0)
    ins = [x2] + ([y] if y is not None else [])
    in_specs = [pl.BlockSpec((tm, D), row)] * len(ins) + [
        pl.BlockSpec((1, 1, D), lay),
        pl.BlockSpec((1, D, 4 * W), lay),
        pl.BlockSpec((1, SW, D), lay),
        pl.BlockSpec((1, D, SW), lay),
        pl.BlockSpec((1, SW, D), lay),
        pl.BlockSpec((1, D, LANES), lay),
        pl.BlockSpec((1, 1, LANES), lay),
        pl.BlockSpec((1, CONV_W, 2 * W), lay),
        pl.BlockSpec((1, 1, 2 * W), lay),
    ]
    return pl.pallas_call(
        functools.partial(_proj_kernel, y is not None, S // tm),
        grid=(N // tm,),
        in_specs=in_specs,
        out_specs=[
            pl.BlockSpec((tm, 2 * W), row),
            pl.BlockSpec((tm, 2 * W), row),
            pl.BlockSpec((tm // P, SW, P), lambda i: (i, 0, 0)),
            pl.BlockSpec((tm, SW), row),
            pl.BlockSpec((tm // P, SW, P), lambda i: (i, 0, 0)),
            pl.BlockSpec((tm, LANES), row),
            pl.BlockSpec((tm // L, 8, L), lambda i: (i, 0, 0)),
        ],
        out_shape=[
            jax.ShapeDtypeStruct((N, 2 * W), BF16),
            jax.ShapeDtypeStruct((N, 2 * W), BF16),
            jax.ShapeDtypeStruct((N // P, SW, P), BF16),
            jax.ShapeDtypeStruct((N, SW), BF16),
            jax.ShapeDtypeStruct((N // P, SW, P), BF16),
            jax.ShapeDtypeStruct((N, LANES), F32),
            jax.ShapeDtypeStruct((N // L, 8, L), F32),
        ],
        scratch_shapes=[pltpu.VMEM((8 + tm, 2 * W), F32)],
        compiler_params=_params(("arbitrary",)),
        name="proj",
    )(*ins, ln_mix, w_ml, w_sqt, w_sk, w_svt, w_g, b_g, conv_w, conv_b)


def _mlstm_kernel(qk_ref, vo_ref, gc_ref, gt_ref, hg_ref, out_ref, s_ref, m_ref):
    L, d, W, H = ML_CHUNK, ML_DHEAD, ML_WIDTH, ML_HEADS
    nb = qk_ref.shape[0]

    @pl.when(pl.program_id(1) == 0)
    def _():
        s_ref[...] = jnp.zeros_like(s_ref)
        m_ref[...] = jnp.zeros_like(m_ref)

    row = lax.broadcasted_iota(I32, (L, L), 0)
    col = lax.broadcasted_iota(I32, (L, L), 1)
    causal = col <= row
    tri = jnp.where(causal, 1.0, 0.0).astype(BF16)
    tri_t = jnp.where(row <= col, 1.0, 0.0).astype(BF16)
    lane = lax.broadcasted_iota(I32, (L, LANES), 1)
    ones_cols = jnp.ones((L, d), BF16)
    ones_sq = jnp.ones((d, d), BF16)
    r2 = lax.broadcasted_iota(I32, (LANES, 2 * d), 0)
    c2 = lax.broadcasted_iota(I32, (LANES, 2 * d), 1)
    pick = [jnp.where(((c2 < d) & (r2 == H + h)) | ((c2 >= d) & (r2 == h)), 1.0, 0.0).astype(BF16)
            for h in range(H)]
    hg = hg_ref[0]
    units = [(bb, h) for bb in range(nb) for h in range(H)]

    gts, bts, xs = [], [], []
    for bb in range(nb):
        gt = gt_ref[bb, 0]
        gc = gc_ref[bb]
        lsc_hi, lsc_lo = _split(-_softplus(-gc))
        bc = _dot(tri, lsc_hi) + _dot(tri, lsc_lo)
        lst_hi, lst_lo = _split(-_softplus(-gt))
        gts.append(gt)
        bts.append(_dot(lst_hi, tri_t) + _dot(lst_lo, tri_t))
        xs.append(_split(jnp.where(lane < H, gc, bc)))

    qbs, kbs, v1s, qks, carried, spread = [], [], [], [], [], []
    for bb, h in units:
        qb = qk_ref[bb, :, h * d:(h + 1) * d]
        kb = qk_ref[bb, :, W + h * d:W + (h + 1) * d]
        v = vo_ref[bb, :, h * d:(h + 1) * d]
        qbs.append(qb)
        kbs.append(kb)
        v1s.append(jnp.concatenate([v, ones_cols], axis=1))
        qks.append(_dot_nt(qb, kb))
        carried.append(_dot(qb, s_ref[bb * H + h].astype(BF16)))
        spread.append(_dot(xs[bb][0], pick[h]) + _dot(xs[bb][1], pick[h]))

    ss, m_ts, w_inters = [], [], []
    for u, (bb, h) in enumerate(units):
        b_c = spread[u][:, :d]
        i_r = gts[bb][h:h + 1, :]
        b_r = bts[bb][H + h:H + h + 1, :]
        logd = jnp.where(causal, b_c - b_r + i_r, -jnp.inf)
        inter = b_c + m_ref[u][:1, :]
        m_t = jnp.maximum(inter, jnp.max(logd, axis=-1, keepdims=True))
        ss.append((qks[u] * jnp.exp(logd - m_t)).astype(BF16))
        m_ts.append(m_t)
        w_inters.append(jnp.exp(inter - m_t))

    intras = [_dot(ss[u], v1s[u]) for u in range(len(units))]

    hhs = []
    for u in range(len(units)):
        w2 = jnp.concatenate([w_inters[u], w_inters[u]], axis=1)
        tot = intras[u] + w2 * carried[u]
        hhs.append(tot[:, :d] / jnp.maximum(jnp.abs(tot[:, d:]), jnp.exp(-m_ts[u])))
    sqs = [_dot((hh * hh).astype(BF16), ones_sq) * (1.0 / d) for hh in hhs]
    for u, (bb, h) in enumerate(units):
        og = vo_ref[bb, :, W + h * d:W + (h + 1) * d].astype(F32)
        hn = hhs[u] * lax.rsqrt(sqs[u] + EPS) * hg[:, h * d:(h + 1) * d]
        out_ref[bb, :, h * d:(h + 1) * d] = (_sigmoid(og) * hn).astype(BF16)

    kws, w_olds, m_news = [], [], []
    for u, (bb, h) in enumerate(units):
        b_c, i_c = spread[u][:, :d], spread[u][:, d:]
        i_r = gts[bb][h:h + 1, :]
        b_r = bts[bb][H + h:H + h + 1, :]
        m_prev = m_ref[u][:1, :1]
        b_last = b_r[:, L - 1:L]
        m_new = jnp.maximum(b_last + m_prev, jnp.max(b_last - b_r + i_r, axis=-1, keepdims=True))
        w_olds.append(jnp.exp(b_last + m_prev - m_new))
        m_news.append(m_new)
        kws.append((kbs[u].astype(F32) * jnp.exp(b_last - b_c + i_c - m_new)).T.astype(BF16))
    for u in range(len(units)):
        s_ref[u] = w_olds[u] * s_ref[u] + _dot(kws[u], v1s[u])
        m_ref[u] = jnp.broadcast_to(m_news[u], (8, LANES))


def _mlstm(l, B, S, qka, vo, gc, gt, ml_head_g):
    L, W, H, d = ML_CHUNK, ML_WIDTH, ML_HEADS, ML_DHEAD
    nc = S // L
    nb = 4 if B % 4 == 0 else (2 if B % 2 == 0 else 1)
    out = pl.pallas_call(
        _mlstm_kernel,
        grid=(B // nb, nc),
        in_specs=[
            pl.BlockSpec((nb, L, 2 * W), lambda b, c: (b, c, 0)),
            pl.BlockSpec((nb, L, 2 * W), lambda b, c: (b, c, 0)),
            pl.BlockSpec((nb, L, LANES), lambda b, c: (b, c, 0)),
            pl.BlockSpec((nb, 1, 8, L), lambda b, c: (b, c, 0, 0)),
            pl.BlockSpec((1, 1, W), lambda b, c: (l, 0, 0)),
        ],
        out_specs=pl.BlockSpec((nb, L, W), lambda b, c: (b, c, 0)),
        out_shape=jax.ShapeDtypeStruct((B, S, W), BF16),
        scratch_shapes=[
            pltpu.VMEM((nb * H, d, 2 * d), F32),
            pltpu.VMEM((nb * H, 8, LANES), F32),
        ],
        compiler_params=_params(("arbitrary", "arbitrary")),
        name="mlstm",
    )(qka.reshape(B, S, 2 * W), vo.reshape(B, S, 2 * W), gc.reshape(B, S, LANES), gt.reshape(B, nc, 8, L),
      ml_head_g)
    return out.reshape(B * S, W)


def _sb_kernel(qt_ref, k_ref, vt_ref, g_ref, o_ref, acc_ref):
    P, H, d = SB_SPAN, SB_HEADS, SB_DHEAD
    i = pl.program_id(1)
    row = lax.broadcasted_iota(I32, (P, P), 0)
    col = lax.broadcasted_iota(I32, (P, P), 1)
    strict = row < col
    neg_from = jnp.where(col >= row, -1.0, 0.0).astype(BF16)

    heads = [slice(h * d, (h + 1) * d) for h in range(H)]

    def spans(js, csums, diagonal):
        units = [(n, j, h) for n, j in enumerate(js) for h in range(H)]
        masked = [diagonal and n == 0 for n, _, _ in units]
        ks = [k_ref[pl.ds(pl.multiple_of(j * P, P), P), heads[h]] for _, j, h in units]
        qts = [qt_ref[0, heads[h], :] for _, _, h in units]
        zs = [_dot(kj, qt) for kj, qt in zip(ks, qts)]
        sps = []
        for z, m in zip(zs, masked):
            sp = jnp.where(z > SP_LINEAR, z, jnp.log(1.0 + jnp.exp2(z)) * LOG2E)
            sps.append(jnp.where(strict, sp, 0.0) if m else sp)
        tails = [_dot(neg_from, sp.astype(BF16)) for sp in sps]
        csums = list(csums) if csums is not None else [None] * H
        parts = [None] * H
        for u, (_, j, h) in enumerate(units):
            e = zs[u] + tails[u]
            total = tails[u][0:1, :]
            if masked[u]:
                a = jnp.where(strict, jnp.exp2(e), 0.0)
                csums[h] = -total
            else:
                a = jnp.exp2(e - csums[h])
                csums[h] = csums[h] - total
            part = _dot(vt_ref[j, heads[h], :], a.astype(BF16))
            parts[h] = part if parts[h] is None else parts[h] + part
        for h in range(H):
            if diagonal:
                acc_ref[heads[h], :] = parts[h]
            else:
                acc_ref[heads[h], :] += parts[h]
        return tuple(csums)

    @pl.when(i == 0)
    def _():
        spans([i], None, True)

    @pl.when(i > 0)
    def _():
        csums = spans([i, i - 1], None, True)
        left = i - 1
        odd = left % 2
        csums = lax.fori_loop(0, odd, lambda jj, c: spans([i - 2], c, False), csums)
        lax.fori_loop(0, left // 2, lambda p, c: spans([i - 2 - odd - 2 * p, i - 3 - odd - 2 * p], c, False), csums)
    g = g_ref[0]
    for h in range(H):
        hs = slice(h * d, (h + 1) * d)
        acc = acc_ref[hs, :]
        on = acc * lax.rsqrt(jnp.mean(acc * acc, axis=0, keepdims=True) + EPS)
        o_ref[:, hs] = (on.T * g[:, hs]).astype(BF16)


def _stickbreak(l, B, S, sqt, sk, svt, sb_head_g):
    P, W = SB_SPAN, SB_WIDTH
    nq = S // P
    return pl.pallas_call(
        _sb_kernel,
        grid=(B, nq),
        in_specs=[
            pl.BlockSpec((1, W, P), lambda b, i: (b * nq + i, 0, 0)),
            pl.BlockSpec((S, W), lambda b, i: (b, 0)),
            pl.BlockSpec((nq, W, P), lambda b, i: (b, 0, 0)),
            pl.BlockSpec((1, 1, W), lambda b, i: (l, 0, 0)),
        ],
        out_specs=pl.BlockSpec((P, W), lambda b, i: (b * nq + i, 0)),
        out_shape=jax.ShapeDtypeStruct((B * S, W), BF16),
        scratch_shapes=[pltpu.VMEM((W, P), F32)],
        compiler_params=_params(("arbitrary", "arbitrary")),
        name="stickbreak",
    )(sqt, sk, svt, sb_head_g)


def _route(lgt, tm):
    G, E = N_GROUPS, EXP_PER_GROUP
    lg = [lgt[i:i + 1, :] for i in range(G + G * E)]
    mg = functools.reduce(jnp.maximum, lg[:G])
    eg = [jnp.exp(v - mg) for v in lg[:G]]
    zg = functools.reduce(jnp.add, eg)
    pg = [v / zg for v in eg]
    p_sel = functools.reduce(jnp.maximum, pg)
    gid = jnp.where(pg[0] == p_sel, 0, jnp.where(pg[1] == p_sel, 1, jnp.where(pg[2] == p_sel, 2, 3)))
    es = [jnp.where(gid == 0, lg[G + e], jnp.where(gid == 1, lg[G + E + e],
          jnp.where(gid == 2, lg[G + 2 * E + e], lg[G + 3 * E + e]))) for e in range(E)]
    me = functools.reduce(jnp.maximum, es)
    ee = [jnp.exp(v - me) for v in es]
    ze = functools.reduce(jnp.add, ee)
    pe = [v / ze for v in ee]

    def first_max(vals):
        top = functools.reduce(jnp.maximum, vals)
        idx = jnp.where(vals[0] == top, 0, jnp.where(vals[1] == top, 1, jnp.where(vals[2] == top, 2, 3)))
        return top, idx

    v1, i1 = first_max(pe)
    v2, i2 = first_max([jnp.where(i1 == e, -1.0, pe[e]) for e in range(E)])
    tsum = v1 + v2
    tw1, tw2 = v1 / tsum, v2 / tsum
    gates = [p_sel * (jnp.where(i1 == e, tw1, 0.0) + jnp.where(i2 == e, tw2, 0.0)) for e in range(E)]

    sub8 = lax.broadcasted_iota(I32, (8, tm), 0)
    onehot = jnp.where(sub8 == gid, 1.0, 0.0)
    r = lax.broadcasted_iota(I32, (SORT_BLOCK, SORT_BLOCK), 0)
    c = lax.broadcasted_iota(I32, (SORT_BLOCK, SORT_BLOCK), 1)
    before = jnp.where(r < c, 1.0, 0.0).astype(BF16)
    lane = lax.broadcasted_iota(I32, (8, LANES), 1)
    counts = jnp.zeros((8, LANES), F32)
    lps = []
    for jj in range(tm // SORT_BLOCK):
        oj = onehot[:, jj * SORT_BLOCK:(jj + 1) * SORT_BLOCK]
        pre = _dot(oj.astype(BF16), before)
        cnt = jnp.sum(oj, axis=-1, keepdims=True)
        pcnt = jnp.floor((cnt + (ROW_TILE_BF16 - 1.0)) * (1.0 / ROW_TILE_BF16)) * ROW_TILE_BF16
        start = jnp.zeros((1, 1), F32)
        lp = jnp.zeros((1, SORT_BLOCK), F32)
        for g in range(G):
            lp = lp + oj[g:g + 1, :] * (pre[g:g + 1, :] + start)
            start = start + pcnt[g:g + 1, :]
        lps.append(lp)
        counts = counts + jnp.where(lane == jj, cnt, 0.0)
    lp = jnp.concatenate(lps, axis=1)
    rows = gates + [gid.astype(F32), lp]
    pack = jnp.zeros((8, tm), F32)
    for kk, v in enumerate(rows):
        pack = pack + jnp.where(sub8 == kk, v, 0.0)
    return pack, counts.astype(I32)


def _post_kernel(has_y, *refs):
    if has_y:
        x_ref, y_ref = refs[:2]
        refs = refs[2:]
        x = x_ref[...] + y_ref[...].astype(F32)
    else:
        x_ref = refs[0]
        refs = refs[1:]
        x = x_ref[...]
    (hml_ref, hsb_ref, wo_ref, gmem_ref, wq_ref, k_ref, v_ref, wmo_ref, gffn_ref, wrh_ref, wrl_ref, br_ref,
     x2_ref, h3_ref, rt_ref, gcol_ref, cnt_ref) = refs
    tm = x.shape[0]
    x1 = x + _dot(hml_ref[...], wo_ref[0, :ML_WIDTH, :]) + _dot(hsb_ref[...], wo_ref[0, ML_WIDTH:, :])
    q = _dot(_rms(x1, gmem_ref[0]).astype(BF16), wq_ref[0]).astype(BF16)
    kk = k_ref[0, 0]
    vv = v_ref[0, 0]
    dh = q.shape[1] // XA_HEADS
    heads = [slice(hd * dh, (hd + 1) * dh) for hd in range(XA_HEADS)]
    scores = [_dot_nt(q[:, sl], kk[:, sl]) * (dh ** -0.5) for sl in heads]
    probs = []
    for s in scores:
        p = jnp.exp(s - jnp.max(s, axis=-1, keepdims=True))
        probs.append((p / jnp.sum(p, axis=-1, keepdims=True)).astype(BF16))
    outs = [_dot(p, vv[:, sl]).astype(BF16) for p, sl in zip(probs, heads)]
    x2 = x1 + _dot(jnp.concatenate(outs, axis=-1), wmo_ref[0])
    x2_ref[...] = x2
    h3_hi, h3_lo = _split(_rms(x2, gffn_ref[0]))
    h3_ref[...] = h3_hi
    wrh = wrh_ref[0]
    both = _dot(h3_hi, jnp.concatenate([wrh, wrl_ref[0]], axis=1))
    lg = both[:, :LANES] + both[:, LANES:] + _dot(h3_lo, wrh) + br_ref[0]
    pack, counts = _route(lg.T, tm)
    rt_ref[...] = pack
    gcol_ref[...] = jnp.concatenate([pack, jnp.zeros((LANES - 8, tm), F32)], axis=0).T
    cnt_ref[0] = counts


def _post(l, B, S, x2, y, hml, hsb, w_out, ln_mem, w_mq, kx, vx, w_mo, ln_ffn, wr_hi, wr_lo, b_r):
    N, D = x2.shape
    M = kx.shape[2]
    tm = TOKEN_TILE
    tpb = S // tm
    row = lambda i: (i, 0)
    lay = lambda i: (l, 0, 0)
    ins = [x2] + ([y] if y is not None else [])
    in_specs = [pl.BlockSpec((tm, D), row)] * len(ins) + [
        pl.BlockSpec((tm, ML_WIDTH), row),
        pl.BlockSpec((tm, SB_WIDTH), row),
        pl.BlockSpec((1, D, D), lay),
        pl.BlockSpec((1, 1, D), lay),
        pl.BlockSpec((1, D, D), lay),
        pl.BlockSpec((1, 1, M, D), lambda i: (l, i // tpb, 0, 0)),
        pl.BlockSpec((1, 1, M, D), lambda i: (l, i // tpb, 0, 0)),
        pl.BlockSpec((1, D, D), lay),
        pl.BlockSpec((1, 1, D), lay),
        pl.BlockSpec((1, D, LANES), lay),
        pl.BlockSpec((1, D, LANES), lay),
        pl.BlockSpec((1, 1, LANES), lay),
    ]
    return pl.pallas_call(
        functools.partial(_post_kernel, y is not None),
        grid=(N // tm,),
        in_specs=in_specs,
        out_specs=[
            pl.BlockSpec((tm, D), row),
            pl.BlockSpec((tm, D), row),
            pl.BlockSpec((8, tm), lambda i: (0, i)),
            pl.BlockSpec((tm, LANES), row),
            pl.BlockSpec((1, 8, LANES), lambda i: (i, 0, 0)),
        ],
        out_shape=[
            jax.ShapeDtypeStruct((N, D), F32),
            jax.ShapeDtypeStruct((N, D), BF16),
            jax.ShapeDtypeStruct((8, N), F32),
            jax.ShapeDtypeStruct((N, LANES), F32),
            jax.ShapeDtypeStruct((N // tm, 8, LANES), I32),
        ],
        compiler_params=_params(("arbitrary",)),
        name="post",
    )(*ins, hml, hsb, w_out, ln_mem, w_mq, kx, vx, w_mo, ln_ffn, wr_hi, wr_lo, b_r)


def _moe_kernel(T, cnt_ref, h3_ref, rt_ref, gcol_ref, w1_ref, w3_ref, w2_ref, y_ref,
                xs_ref, gs_ref, ys_ref, sl_ref, gl_ref):
    G, CH, CH_TAIL, SBK, WIN, RT = N_GROUPS, MOE_CHUNK, MOE_CHUNK_TAIL, SORT_BLOCK, SORT_WIN, ROW_TILE_BF16
    b, g, hf = pl.program_id(0), pl.program_id(1), pl.program_id(2)
    nsb = T // SBK
    nbat = sl_ref.shape[0]
    n_half = pl.num_programs(2)
    D = h3_ref.shape[1]

    pc = [[((cnt_ref[(b * nsb + j) * G + gg] + (RT - 1)) // RT) * RT for j in range(nsb)] for gg in range(G)]
    base = [0]
    for gg in range(G):
        base.append(base[-1] + functools.reduce(lambda u, w: u + w, pc[gg]))

    def seg_starts(j):
        out, src = [], 0
        for gg in range(G):
            dst = base[gg]
            for jp in range(j):
                dst = dst + pc[gg][jp]
            out.append((src, dst))
            src = src + pc[gg][j]
        return out

    @pl.when((g == 0) & (hf == 0))
    def _sort():
        total = pl.multiple_of(base[G], RT)
        xs_ref[pl.ds(total, CH), :] = jnp.zeros((CH, D), BF16)
        gs_ref[pl.ds(total, CH), :] = jnp.zeros((CH, LANES), F32)
        riota = lax.broadcasted_iota(I32, (WIN, SBK), 0)
        for j0 in range(0, nsb, nbat):
            for jj in range(nbat):
                j = j0 + jj
                lp = rt_ref[5:6, j * SBK:(j + 1) * SBK].astype(I32)
                perm = jnp.where(riota == lp, 1.0, 0.0).astype(BF16)
                sl_ref[jj] = _dot(perm, h3_ref[j * SBK:(j + 1) * SBK, :]).astype(BF16)
                g_hi, g_lo = _split(gcol_ref[j * SBK:(j + 1) * SBK, :])
                both = _dot(perm, jnp.concatenate([g_hi, g_lo], axis=1))
                gl_ref[jj] = both[:, :LANES] + both[:, LANES:]
            for jj in range(nbat):
                for gg, (src, dst) in enumerate(seg_starts(j0 + jj)):
                    def copy_in(c, carry, jj=jj, src=src, dst=dst):
                        s = pl.multiple_of(src + c * RT, RT)
                        t = pl.multiple_of(dst + c * RT, RT)
                        xs_ref[pl.ds(t, RT), :] = sl_ref[jj, pl.ds(s, RT), :]
                        gs_ref[pl.ds(t, RT), :] = gl_ref[jj, pl.ds(s, RT), :]
                        return carry
                    lax.fori_loop(0, pc[gg][j0 + jj] // RT, copy_in, 0)

    rows_g = base[1] - base[0]
    base_g = base[0]
    for gg in range(1, G):
        rows_g = jnp.where(g == gg, base[gg + 1] - base[gg], rows_g)
        base_g = jnp.where(g == gg, base[gg], base_g)

    def run_chunk(r0, rows):
        r0 = pl.multiple_of(r0, RT)
        xc = xs_ref[pl.ds(r0, rows), :]
        gc = gs_ref[pl.ds(r0, rows), :]
        ups = [(_dot(xc, w1_ref[e]), _dot(xc, w3_ref[e])) for e in range(EXPERTS_PER_STEP)]
        hms = []
        for e, (a, u) in enumerate(ups):
            gate = gc[:, e:e + 1]
            for k in range(1, EXP_PER_GROUP // EXPERTS_PER_STEP):
                gate = jnp.where(hf == k, gc[:, k * EXPERTS_PER_STEP + e:k * EXPERTS_PER_STEP + e + 1], gate)
            hms.append(((a * _sigmoid(a)) * u * gate).astype(BF16))
        acc = functools.reduce(jnp.add, [_dot(hm, w2_ref[e]) for e, hm in enumerate(hms)])

        @pl.when(hf == 0)
        def _():
            ys_ref[pl.ds(r0, rows), :] = acc

        @pl.when(hf != 0)
        def _():
            ys_ref[pl.ds(r0, rows), :] += acc

    big = 2 * CH
    n_big = rows_g // big
    rest = rows_g - n_big * big
    merged = jnp.where((n_big >= 1) & (rest > 0) & (rest <= CH_TAIL), 1, 0)
    start, left = base_g, rows_g
    for size, n in ((big, n_big - merged), (big + CH_TAIL, merged), (CH, None)):
        if n is None:
            n = left // size

        def body(c, carry, start=start, size=size):
            run_chunk(start + c * size, size)
            return carry

        lax.fori_loop(0, n, body, 0)
        start, left = start + n * size, jnp.maximum(left - n * size, 0)

    def tail_body(c, carry, start=start):
        run_chunk(start + c * CH_TAIL, CH_TAIL)
        return carry

    lax.fori_loop(0, (left + (CH_TAIL - 1)) // CH_TAIL, tail_body, 0)

    @pl.when((g == G - 1) & (hf == n_half - 1))
    def _unsort():
        ciota = lax.broadcasted_iota(I32, (SBK, WIN), 1)
        for j0 in range(0, nsb, nbat):
            for jj in range(nbat):
                for gg, (src, dst) in enumerate(seg_starts(j0 + jj)):
                    def copy_out(c, carry, jj=jj, src=src, dst=dst):
                        s = pl.multiple_of(src + c * RT, RT)
                        t = pl.multiple_of(dst + c * RT, RT)
                        sl_ref[jj, pl.ds(s, RT), :] = ys_ref[pl.ds(t, RT), :].astype(BF16)
                        return carry
                    lax.fori_loop(0, pc[gg][j0 + jj] // RT, copy_out, 0)
            for jj in range(nbat):
                j = j0 + jj
                lp = gcol_ref[j * SBK:(j + 1) * SBK, 5:6].astype(I32)
                perm_t = jnp.where(ciota == lp, 1.0, 0.0).astype(BF16)
                y_ref[j * SBK:(j + 1) * SBK, :] = _dot(perm_t, sl_ref[jj]).astype(BF16)


def _moe(l, T, counts, h3, rt, gcol, w1, w3, w2):
    N, D = h3.shape
    G, E, EPS_ = N_GROUPS, EXP_PER_GROUP, EXPERTS_PER_STEP
    F = w1.shape[-1]
    rows = T + (T // SORT_BLOCK) * G * (ROW_TILE_BF16 - 1) + MOE_CHUNK
    rows = -(-rows // ROW_TILE_BF16) * ROW_TILE_BF16
    nsb = T // SORT_BLOCK
    nbat = next(n for n in (SORT_BATCH, 2, 1) if nsb % n == 0)
    grid_spec = pltpu.PrefetchScalarGridSpec(
        num_scalar_prefetch=1,
        grid=(N // T, G, E // EPS_),
        in_specs=[
            pl.BlockSpec((T, D), lambda b, g, hf, cnt: (b, 0)),
            pl.BlockSpec((8, T), lambda b, g, hf, cnt: (0, b)),
            pl.BlockSpec((T, LANES), lambda b, g, hf, cnt: (b, 0)),
            pl.BlockSpec((None, None, EPS_, D, F), lambda b, g, hf, cnt: (l, g, hf, 0, 0)),
            pl.BlockSpec((None, None, EPS_, D, F), lambda b, g, hf, cnt: (l, g, hf, 0, 0)),
            pl.BlockSpec((None, None, EPS_, F, D), lambda b, g, hf, cnt: (l, g, hf, 0, 0)),
        ],
        out_specs=pl.BlockSpec((T, D), lambda b, g, hf, cnt: (b, 0)),
        scratch_shapes=[
            pltpu.VMEM((rows, D), BF16),
            pltpu.VMEM((rows, LANES), F32),
            pltpu.VMEM((rows, D), F32),
            pltpu.VMEM((nbat, SORT_WIN, D), BF16),
            pltpu.VMEM((nbat, SORT_WIN, LANES), F32),
        ],
    )
    return pl.pallas_call(
        functools.partial(_moe_kernel, T),
        grid_spec=grid_spec,
        out_shape=jax.ShapeDtypeStruct((N, D), BF16),
        compiler_params=_params(("arbitrary", "arbitrary", "arbitrary")),
        name="moe",
    )(counts, h3, rt, gcol, w1, w3, w2)


def _final_kernel(x_ref, y_ref, g_ref, o_ref):
    o_ref[...] = _rms(x_ref[...] + y_ref[...].astype(F32), g_ref[...])


def _final(x2, y, ln_final):
    N, D = x2.shape
    tm = TOKEN_TILE
    return pl.pallas_call(
        _final_kernel,
        grid=(N // tm,),
        in_specs=[pl.BlockSpec((tm, D), lambda i: (i, 0)), pl.BlockSpec((tm, D), lambda i: (i, 0)),
                  pl.BlockSpec((1, D), lambda i: (0, 0))],
        out_specs=pl.BlockSpec((tm, D), lambda i: (i, 0)),
        out_shape=jax.ShapeDtypeStruct((N, D), F32),
        compiler_params=_params(("arbitrary",)),
        name="final_norm",
    )(x2, y, ln_final.reshape(1, D))


def kernel(x, mem, ln_mix, w_in, conv_w, conv_b, i_bias, f_bias, ml_head_g, sb_head_g, w_out, ln_mem, ln_memkv,
           w_mq, w_mk, w_mv, w_mo, ln_ffn, w_rg, b_rg, w_re, b_re, w_e1, w_e3, w_e2, ln_final):
    B, S, D = x.shape
    N = B * S
    depth = w_in.shape[0]
    G, E = N_GROUPS, EXP_PER_GROUP
    assert S % TOKEN_TILE == 0 and TOKEN_TILE % ML_CHUNK == 0 and TOKEN_TILE % SORT_BLOCK == 0 and S % SB_SPAN == 0
    T = 2048 if S % 2048 == 0 else S

    c0, c1 = 4 * ML_WIDTH, 4 * ML_WIDTH + 2 * ML_HEADS
    w_ml = w_in[:, :, :c0].astype(BF16)
    w_sqt = jnp.swapaxes(w_in[:, :, c1:c1 + SB_WIDTH], 1, 2).astype(BF16)
    w_sk = w_in[:, :, c1 + SB_WIDTH:c1 + 2 * SB_WIDTH].astype(BF16)
    w_svt = jnp.swapaxes(w_in[:, :, c1 + 2 * SB_WIDTH:], 1, 2).astype(BF16)
    w_gate = w_in[:, :, c0:c1]
    w_g = jnp.pad(w_gate, ((0, 0), (0, 0), (0, LANES - 8))).astype(BF16)
    gate_b = jnp.concatenate([i_bias, f_bias], axis=-1)
    b_g = jnp.pad(gate_b, ((0, 0), (0, LANES - 8))).reshape(depth, 1, LANES)
    w_r = jnp.pad(jnp.concatenate([w_rg, w_re], axis=-1), ((0, 0), (0, 0), (0, LANES - G - G * E)))
    wr_hi = w_r.astype(BF16)
    wr_lo = (w_r - wr_hi.astype(F32)).astype(BF16)
    b_r = jnp.pad(jnp.concatenate([b_rg, b_re], axis=-1), ((0, 0), (0, LANES - G - G * E))).reshape(depth, 1, LANES)
    w_out_b, w_mq_b, w_mo_b = w_out.astype(BF16), w_mq.astype(BF16), w_mo.astype(BF16)
    w1, w3, w2 = w_e1.astype(BF16), w_e3.astype(BF16), w_e2.astype(BF16)
    r3 = lambda a: a.reshape(depth, 1, a.shape[-1])

    kx, vx = _memkv(mem, ln_memkv, w_mk.astype(BF16), w_mv.astype(BF16))

    x2, y = x.reshape(N, D), None
    nsub = TOKEN_TILE // SORT_BLOCK
    for l in range(depth):
        qka, vo, sqt, sk, svt, gc, gt = _proj(l, S, x2, y, r3(ln_mix), w_ml, w_sqt, w_sk, w_svt, w_g, b_g,
                                              conv_w, r3(conv_b))
        hml = _mlstm(l, B, S, qka, vo, gc, gt, r3(ml_head_g))
        hsb = _stickbreak(l, B, S, sqt, sk, svt, r3(sb_head_g))
        x2, h3, rt, gcol, cnt = _post(l, B, S, x2, y, hml, hsb, w_out_b, r3(ln_mem), w_mq_b, kx, vx, w_mo_b,
                                      r3(ln_ffn), wr_hi, wr_lo, b_r)
        counts = jnp.swapaxes(cnt[:, :G, :nsub], 1, 2).reshape(-1)
        y = _moe(l, T, counts, h3, rt, gcol, w1, w3, w2)
    return _final(x2, y, ln_final).reshape(B, S, D)
```

```python
import functools

import jax
import jax.numpy as jnp
from jax import lax
from jax.experimental import pallas as pl
from jax.experimental.pallas import tpu as pltpu

F32 = jnp.float32
BF16 = jnp.bfloat16
I32 = jnp.int32

ML_HEADS = 4
ML_DHEAD = 128
SB_HEADS = 4
SB_DHEAD = 128
ML_WIDTH = ML_HEADS * ML_DHEAD
SB_WIDTH = SB_HEADS * SB_DHEAD
CONV_W = 4
ML_CHUNK = 128
SB_SPAN = 256
XA_HEADS = 4
N_GROUPS = 4
EXP_PER_GROUP = 4
EPS = 1e-6
LOG2E = 1.4426950408889634
SP_LINEAR = 100.0

LANES = 128
ROW_TILE_BF16 = 16
SORT_BLOCK = 256
SORT_WIN = 384
SORT_BATCH = 4
MOE_CHUNK = 256
MOE_CHUNK_TAIL = 128
EXPERTS_PER_STEP = 2
TOKEN_TILE = 1024
VMEM_LIMIT = 56 * 1024 * 1024


def _dot(a, b):
    return jnp.dot(a, b, preferred_element_type=F32)


def _dot_nt(a, b):
    return lax.dot_general(a, b, (((1,), (1,)), ((), ())), preferred_element_type=F32)


def _split(x):
    hi = x.astype(BF16)
    lo = (x - hi.astype(F32)).astype(BF16)
    return hi, lo


def _rms(x, g):
    return x * lax.rsqrt(jnp.mean(x * x, axis=-1, keepdims=True) + EPS) * g


def _sigmoid(x):
    return 0.5 * jnp.tanh(0.5 * x) + 0.5


def _softplus(x):
    return jnp.maximum(x, 0.0) + jnp.log1p(jnp.exp(-jnp.abs(x)))


def _params(sem):
    return pltpu.CompilerParams(dimension_semantics=sem, vmem_limit_bytes=VMEM_LIMIT)


def _memkv_kernel(mem_ref, g_ref, wk_ref, wv_ref, k_ref, v_ref):
    nb, M, D = mem_ref.shape
    h = _rms(mem_ref[...].reshape(nb * M, D), g_ref[0]).astype(BF16)
    k_ref[0] = _dot(h, wk_ref[0]).astype(BF16).reshape(nb, M, D)
    v_ref[0] = _dot(h, wv_ref[0]).astype(BF16).reshape(nb, M, D)


def _memkv(mem, ln_memkv, w_mk, w_mv):
    B, M, D = mem.shape
    depth = w_mk.shape[0]
    nb = 4 if B % 4 == 0 else (2 if B % 2 == 0 else 1)
    return pl.pallas_call(
        _memkv_kernel,
        grid=(depth, B // nb),
        in_specs=[
            pl.BlockSpec((nb, M, D), lambda l, b: (b, 0, 0)),
            pl.BlockSpec((1, 1, D), lambda l, b: (l, 0, 0)),
            pl.BlockSpec((1, D, D), lambda l, b: (l, 0, 0)),
            pl.BlockSpec((1, D, D), lambda l, b: (l, 0, 0)),
        ],
        out_specs=[
            pl.BlockSpec((1, nb, M, D), lambda l, b: (l, b, 0, 0)),
            pl.BlockSpec((1, nb, M, D), lambda l, b: (l, b, 0, 0)),
        ],
        out_shape=[jax.ShapeDtypeStruct((depth, B, M, D), BF16)] * 2,
        compiler_params=_params(("arbitrary", "arbitrary")),
        name="memkv",
    )(mem, ln_memkv.reshape(depth, 1, D), w_mk, w_mv)


def _proj_kernel(has_y, tiles_per_seq, *refs):
    if has_y:
        x_ref, y_ref = refs[:2]
        refs = refs[2:]
        x = x_ref[...] + y_ref[...].astype(F32)
    else:
        x_ref = refs[0]
        refs = refs[1:]
        x = x_ref[...]
    (g_ref, wml_ref, wsqt_ref, wsk_ref, wsvt_ref, wg_ref, bg_ref, cw_ref, cb_ref,
     qka_ref, vo_ref, sqt_ref, sk_ref, svt_ref, gc_ref, gt_ref, qk_ref) = refs
    tm, W, L, P = x.shape[0], ML_WIDTH, ML_CHUNK, SB_SPAN

    @pl.when(pl.program_id(0) % tiles_per_seq == 0)
    def _():
        qk_ref[:8, :] = jnp.zeros((8, 2 * W), F32)

    h = _rms(x, g_ref[0]).astype(BF16)
    cw = cw_ref[0]
    cb = cb_ref[0]
    cblk = 2 * LANES

    def conv_block(c):
        rows = L
        for r in range(0, tm, rows):
            acc = cb[:, c:c + cblk]
            for tap in range(CONV_W):
                lo = 8 - (CONV_W - 1) + tap + r
                acc = acc + qk_ref[lo:lo + rows, c:c + cblk] * cw[tap:tap + 1, c:c + cblk]
            qa = acc * _sigmoid(acc)
            if c >= W:
                qa = qa * (ML_DHEAD ** -0.5)
            qka_ref[r:r + rows, c:c + cblk] = qa.astype(BF16)
        qk_ref[:8, c:c + cblk] = qk_ref[tm:, c:c + cblk]

    def store_v():
        vo_ref[:, :W] = _dot(h, wml_ref[0, :, 2 * W:3 * W]).astype(BF16)

    def store_o():
        vo_ref[:, W:] = _dot(h, wml_ref[0, :, 3 * W:]).astype(BF16)

    def store_sk():
        sk_ref[...] = _dot(h, wsk_ref[0]).astype(BF16)

    def store_sqt():
        sqt = (_dot_nt(wsqt_ref[0], h) * (SB_DHEAD ** -0.5 * LOG2E)).astype(BF16)
        for c in range(tm // P):
            sqt_ref[c] = sqt[:, c * P:(c + 1) * P]

    def store_svt():
        svt = _dot_nt(wsvt_ref[0], h).astype(BF16)
        for c in range(tm // P):
            svt_ref[c] = svt[:, c * P:(c + 1) * P]

    others = [store_v, store_o, store_sk, store_sqt, store_svt]
    blocks = list(range(0, 2 * W, cblk))
    for n, c in enumerate(blocks):
        qk_ref[8:, c:c + cblk] = _dot(h, wml_ref[0, :, c:c + cblk])
        if n > 0:
            conv_block(blocks[n - 1])
        if others:
            others.pop(0)()
    conv_block(blocks[-1])
    for rest in others:
        rest()
    gc = _dot(h, wg_ref[0]) + bg_ref[0]
    gc_ref[...] = gc
    gt = gc.T
    for c in range(tm // L):
        gt_ref[c] = gt[:8, c * L:(c + 1) * L]


def _proj(l, S, x2, y, ln_mix, w_ml, w_sqt, w_sk, w_svt, w_g, b_g, conv_w, conv_b):
    N, D = x2.shape
    tm, W, L, P, SW = TOKEN_TILE, ML_WIDTH, ML_CHUNK, SB_SPAN, SB_WIDTH
    row = lambda i: (i, 0)
    lay = lambda i: (l, 0, 0)
    ins = [x2] + ([y] if y is not None else [])
    in_specs = [pl.BlockSpec((tm, D), row)] * len(ins) + [
        pl.BlockSpec((1, 1, D), lay),
        pl.BlockSpec((1, D, 4 * W), lay),
        pl.BlockSpec((1, SW, D), lay),
        pl.BlockSpec((1, D, SW), lay),
        pl.BlockSpec((1, SW, D), lay),
        pl.BlockSpec((1, D, LANES), lay),
        pl.BlockSpec((1, 1, LANES), lay),
        pl.BlockSpec((1, CONV_W, 2 * W), lay),
        pl.BlockSpec((1, 1, 2 * W), lay),
    ]
    return pl.pallas_call(
        functools.partial(_proj_kernel, y is not None, S // tm),
        grid=(N // tm,),
        in_specs=in_specs,
        out_specs=[
            pl.BlockSpec((tm, 2 * W), row),
            pl.BlockSpec((tm, 2 * W), row),
            pl.BlockSpec((tm // P, SW, P), lambda i: (i, 0, 0)),
            pl.BlockSpec((tm, SW), row),
            pl.BlockSpec((tm // P, SW, P), lambda i: (i, 0, 0)),
            pl.BlockSpec((tm, LANES), row),
            pl.BlockSpec((tm // L, 8, L), lambda i: (i, 0, 0)),
        ],
        out_shape=[
            jax.ShapeDtypeStruct((N, 2 * W), BF16),
            jax.ShapeDtypeStruct((N, 2 * W), BF16),
            jax.ShapeDtypeStruct((N // P, SW, P), BF16),
            jax.ShapeDtypeStruct((N, SW), BF16),
            jax.ShapeDtypeStruct((N // P, SW, P), BF16),
            jax.ShapeDtypeStruct((N, LANES), F32),
            jax.ShapeDtypeStruct((N // L, 8, L), F32),
        ],
        scratch_shapes=[pltpu.VMEM((8 + tm, 2 * W), F32)],
        compiler_params=_params(("arbitrary",)),
        name="proj",
    )(*ins, ln_mix, w_ml, w_sqt, w_sk, w_svt, w_g, b_g, conv_w, conv_b)


def _mlstm_kernel(qk_ref, vo_ref, gc_ref, gt_ref, hg_ref, out_ref, s_ref, m_ref):
    L, d, W, H = ML_CHUNK, ML_DHEAD, ML_WIDTH, ML_HEADS
    nb = qk_ref.shape[0]

    @pl.when(pl.program_id(1) == 0)
    def _():
        s_ref[...] = jnp.zeros_like(s_ref)
        m_ref[...] = jnp.zeros_like(m_ref)

    row = lax.broadcasted_iota(I32, (L, L), 0)
    col = lax.broadcasted_iota(I32, (L, L), 1)
    causal = col <= row
    tri = jnp.where(causal, 1.0, 0.0).astype(BF16)
    tri_t = jnp.where(row <= col, 1.0, 0.0).astype(BF16)
    lane = lax.broadcasted_iota(I32, (L, LANES), 1)
    ones_cols = jnp.ones((L, d), BF16)
    ones_sq = jnp.ones((d, d), BF16)
    r2 = lax.broadcasted_iota(I32, (LANES, 2 * d), 0)
    c2 = lax.broadcasted_iota(I32, (LANES, 2 * d), 1)
    pick = [jnp.where(((c2 < d) & (r2 == H + h)) | ((c2 >= d) & (r2 == h)), 1.0, 0.0).astype(BF16)
            for h in range(H)]
    hg = hg_ref[0]
    units = [(bb, h) for bb in range(nb) for h in range(H)]

    gts, bts, xs = [], [], []
    for bb in range(nb):
        gt = gt_ref[bb, 0]
        gc = gc_ref[bb]
        lsc_hi, lsc_lo = _split(-_softplus(-gc))
        bc = _dot(tri, lsc_hi) + _dot(tri, lsc_lo)
        lst_hi, lst_lo = _split(-_softplus(-gt))
        gts.append(gt)
        bts.append(_dot(lst_hi, tri_t) + _dot(lst_lo, tri_t))
        xs.append(_split(jnp.where(lane < H, gc, bc)))

    qbs, kbs, v1s, qks, carried, spread = [], [], [], [], [], []
    for bb, h in units:
        qb = qk_ref[bb, :, h * d:(h + 1) * d]
        kb = qk_ref[bb, :, W + h * d:W + (h + 1) * d]
        v = vo_ref[bb, :, h * d:(h + 1) * d]
        qbs.append(qb)
        kbs.append(kb)
        v1s.append(jnp.concatenate([v, ones_cols], axis=1))
        qks.append(_dot_nt(qb, kb))
        carried.append(_dot(qb, s_ref[bb * H + h].astype(BF16)))
        spread.append(_dot(xs[bb][0], pick[h]) + _dot(xs[bb][1], pick[h]))

    ss, m_ts, w_inters = [], [], []
    for u, (bb, h) in enumerate(units):
        b_c = spread[u][:, :d]
        i_r = gts[bb][h:h + 1, :]
        b_r = bts[bb][H + h:H + h + 1, :]
        logd = jnp.where(causal, b_c - b_r + i_r, -jnp.inf)
        inter = b_c + m_ref[u][:1, :]
        m_t = jnp.maximum(inter, jnp.max(logd, axis=-1, keepdims=True))
        ss.append((qks[u] * jnp.exp(logd - m_t)).astype(BF16))
        m_ts.append(m_t)
        w_inters.append(jnp.exp(inter - m_t))

    intras = [_dot(ss[u], v1s[u]) for u in range(len(units))]

    hhs = []
    for u in range(len(units)):
        w2 = jnp.concatenate([w_inters[u], w_inters[u]], axis=1)
        tot = intras[u] + w2 * carried[u]
        hhs.append(tot[:, :d] / jnp.maximum(jnp.abs(tot[:, d:]), jnp.exp(-m_ts[u])))
    sqs = [_dot((hh * hh).astype(BF16), ones_sq) * (1.0 / d) for hh in hhs]
    for u, (bb, h) in enumerate(units):
        og = vo_ref[bb, :, W + h * d:W + (h + 1) * d].astype(F32)
        hn = hhs[u] * lax.rsqrt(sqs[u] + EPS) * hg[:, h * d:(h + 1) * d]
        out_ref[bb, :, h * d:(h + 1) * d] = (_sigmoid(og) * hn).astype(BF16)

    kws, w_olds, m_news = [], [], []
    for u, (bb, h) in enumerate(units):
        b_c, i_c = spread[u][:, :d], spread[u][:, d:]
        i_r = gts[bb][h:h + 1, :]
        b_r = bts[bb][H + h:H + h + 1, :]
        m_prev = m_ref[u][:1, :1]
        b_last = b_r[:, L - 1:L]
        m_new = jnp.maximum(b_last + m_prev, jnp.max(b_last - b_r + i_r, axis=-1, keepdims=True))
        w_olds.append(jnp.exp(b_last + m_prev - m_new))
        m_news.append(m_new)
        kws.append((kbs[u].astype(F32) * jnp.exp(b_last - b_c + i_c - m_new)).T.astype(BF16))
    for u in range(len(units)):
        s_ref[u] = w_olds[u] * s_ref[u] + _dot(kws[u], v1s[u])
        m_ref[u] = jnp.broadcast_to(m_news[u], (8, LANES))


def _mlstm(l, B, S, qka, vo, gc, gt, ml_head_g):
    L, W, H, d = ML_CHUNK, ML_WIDTH, ML_HEADS, ML_DHEAD
    nc = S // L
    nb = 4 if B % 4 == 0 else (2 if B % 2 == 0 else 1)
    out = pl.pallas_call(
        _mlstm_kernel,
        grid=(B // nb, nc),
        in_specs=[
            pl.BlockSpec((nb, L, 2 * W), lambda b, c: (b, c, 0)),
            pl.BlockSpec((nb, L, 2 * W), lambda b, c: (b, c, 0)),
            pl.BlockSpec((nb, L, LANES), lambda b, c: (b, c, 0)),
            pl.BlockSpec((nb, 1, 8, L), lambda b, c: (b, c, 0, 0)),
            pl.BlockSpec((1, 1, W), lambda b, c: (l, 0, 0)),
        ],
        out_specs=pl.BlockSpec((nb, L, W), lambda b, c: (b, c, 0)),
        out_shape=jax.ShapeDtypeStruct((B, S, W), BF16),
        scratch_shapes=[
            pltpu.VMEM((nb * H, d, 2 * d), F32),
            pltpu.VMEM((nb * H, 8, LANES), F32),
        ],
        compiler_params=_params(("arbitrary", "arbitrary")),
        name="mlstm",
    )(qka.reshape(B, S, 2 * W), vo.reshape(B, S, 2 * W), gc.reshape(B, S, LANES), gt.reshape(B, nc, 8, L),
      ml_head_g)
    return out.reshape(B * S, W)


def _sb_kernel(qt_ref, k_ref, vt_ref, g_ref, o_ref, acc_ref):
    P, H, d = SB_SPAN, SB_HEADS, SB_DHEAD
    i = pl.program_id(1)
    row = lax.broadcasted_iota(I32, (P, P), 0)
    col = lax.broadcasted_iota(I32, (P, P), 1)
    strict = row < col
    neg_from = jnp.where(col >= row, -1.0, 0.0).astype(BF16)

    heads = [slice(h * d, (h + 1) * d) for h in range(H)]

    def spans(js, csums, diagonal):
        units = [(n, j, h) for n, j in enumerate(js) for h in range(H)]
        masked = [diagonal and n == 0 for n, _, _ in units]
        ks = [k_ref[pl.ds(pl.multiple_of(j * P, P), P), heads[h]] for _, j, h in units]
        qts = [qt_ref[0, heads[h], :] for _, _, h in units]
        zs = [_dot(kj, qt) for kj, qt in zip(ks, qts)]
        sps = []
        for z, m in zip(zs, masked):
            sp = jnp.where(z > SP_LINEAR, z, jnp.log(1.0 + jnp.exp2(z)) * LOG2E)
            sps.append(jnp.where(strict, sp, 0.0) if m else sp)
        tails = [_dot(neg_from, sp.astype(BF16)) for sp in sps]
        csums = list(csums) if csums is not None else [None] * H
        parts = [None] * H
        for u, (_, j, h) in enumerate(units):
            e = zs[u] + tails[u]
            total = tails[u][0:1, :]
            if masked[u]:
                a = jnp.where(strict, jnp.exp2(e), 0.0)
                csums[h] = -total
            else:
                a = jnp.exp2(e - csums[h])
                csums[h] = csums[h] - total
            part = _dot(vt_ref[j, heads[h], :], a.astype(BF16))
            parts[h] = part if parts[h] is None else parts[h] + part
        for h in range(H):
            if diagonal:
                acc_ref[heads[h], :] = parts[h]
            else:
                acc_ref[heads[h], :] += parts[h]
        return tuple(csums)

    @pl.when(i == 0)
    def _():
        spans([i], None, True)

    @pl.when(i > 0)
    def _():
        csums = spans([i, i - 1], None, True)
        left = i - 1
        odd = left % 2
        csums = lax.fori_loop(0, odd, lambda jj, c: spans([i - 2], c, False), csums)
        lax.fori_loop(0, left // 2, lambda p, c: spans([i - 2 - odd - 2 * p, i - 3 - odd - 2 * p], c, False), csums)
    g = g_ref[0]
    for h in range(H):
        hs = slice(h * d, (h + 1) * d)
        acc = acc_ref[hs, :]
        on = acc * lax.rsqrt(jnp.mean(acc * acc, axis=0, keepdims=True) + EPS)
        o_ref[:, hs] = (on.T * g[:, hs]).astype(BF16)


def _stickbreak(l, B, S, sqt, sk, svt, sb_head_g):
    P, W = SB_SPAN, SB_WIDTH
    nq = S // P
    return pl.pallas_call(
        _sb_kernel,
        grid=(B, nq),
        in_specs=[
            pl.BlockSpec((1, W, P), lambda b, i: (b * nq + i, 0, 0)),
            pl.BlockSpec((S, W), lambda b, i: (b, 0)),
            pl.BlockSpec((nq, W, P), lambda b, i: (b, 0, 0)),
            pl.BlockSpec((1, 1, W), lambda b, i: (l, 0, 0)),
        ],
        out_specs=pl.BlockSpec((P, W), lambda b, i: (b * nq + i, 0)),
        out_shape=jax.ShapeDtypeStruct((B * S, W), BF16),
        scratch_shapes=[pltpu.VMEM((W, P), F32)],
        compiler_params=_params(("arbitrary", "arbitrary")),
        name="stickbreak",
    )(sqt, sk, svt, sb_head_g)


def _route(lgt, tm):
    G, E = N_GROUPS, EXP_PER_GROUP
    lg = [lgt[i:i + 1, :] for i in range(G + G * E)]
    mg = functools.reduce(jnp.maximum, lg[:G])
    eg = [jnp.exp(v - mg) for v in lg[:G]]
    zg = functools.reduce(jnp.add, eg)
    pg = [v / zg for v in eg]
    p_sel = functools.reduce(jnp.maximum, pg)
    gid = jnp.where(pg[0] == p_sel, 0, jnp.where(pg[1] == p_sel, 1, jnp.where(pg[2] == p_sel, 2, 3)))
    es = [jnp.where(gid == 0, lg[G + e], jnp.where(gid == 1, lg[G + E + e],
          jnp.where(gid == 2, lg[G + 2 * E + e], lg[G + 3 * E + e]))) for e in range(E)]
    me = functools.reduce(jnp.maximum, es)
    ee = [jnp.exp(v - me) for v in es]
    ze = functools.reduce(jnp.add, ee)
    pe = [v / ze for v in ee]

    def first_max(vals):
        top = functools.reduce(jnp.maximum, vals)
        idx = jnp.where(vals[0] == top, 0, jnp.where(vals[1] == top, 1, jnp.where(vals[2] == top, 2, 3)))
        return top, idx

    v1, i1 = first_max(pe)
    v2, i2 = first_max([jnp.where(i1 == e, -1.0, pe[e]) for e in range(E)])
    tsum = v1 + v2
    tw1, tw2 = v1 / tsum, v2 / tsum
    gates = [p_sel * (jnp.where(i1 == e, tw1, 0.0) + jnp.where(i2 == e, tw2, 0.0)) for e in range(E)]

    sub8 = lax.broadcasted_iota(I32, (8, tm), 0)
    onehot = jnp.where(sub8 == gid, 1.0, 0.0)
    r = lax.broadcasted_iota(I32, (SORT_BLOCK, SORT_BLOCK), 0)
    c = lax.broadcasted_iota(I32, (SORT_BLOCK, SORT_BLOCK), 1)
    before = jnp.where(r < c, 1.0, 0.0).astype(BF16)
    lane = lax.broadcasted_iota(I32, (8, LANES), 1)
    counts = jnp.zeros((8, LANES), F32)
    lps = []
    for jj in range(tm // SORT_BLOCK):
        oj = onehot[:, jj * SORT_BLOCK:(jj + 1) * SORT_BLOCK]
        pre = _dot(oj.astype(BF16), before)
        cnt = jnp.sum(oj, axis=-1, keepdims=True)
        pcnt = jnp.floor((cnt + (ROW_TILE_BF16 - 1.0)) * (1.0 / ROW_TILE_BF16)) * ROW_TILE_BF16
        start = jnp.zeros((1, 1), F32)
        lp = jnp.zeros((1, SORT_BLOCK), F32)
        for g in range(G):
            lp = lp + oj[g:g + 1, :] * (pre[g:g + 1, :] + start)
            start = start + pcnt[g:g + 1, :]
        lps.append(lp)
        counts = counts + jnp.where(lane == jj, cnt, 0.0)
    lp = jnp.concatenate(lps, axis=1)
    rows = gates + [gid.astype(F32), lp]
    pack = jnp.zeros((8, tm), F32)
    for kk, v in enumerate(rows):
        pack = pack + jnp.where(sub8 == kk, v, 0.0)
    return pack, counts.astype(I32)


def _post_kernel(has_y, *refs):
    if has_y:
        x_ref, y_ref = refs[:2]
        refs = refs[2:]
        x = x_ref[...] + y_ref[...].astype(F32)
    else:
        x_ref = refs[0]
        refs = refs[1:]
        x = x_ref[...]
    (hml_ref, hsb_ref, wo_ref, gmem_ref, wq_ref, k_ref, v_ref, wmo_ref, gffn_ref, wrh_ref, wrl_ref, br_ref,
     x2_ref, h3_ref, rt_ref, gcol_ref, cnt_ref) = refs
    tm = x.shape[0]
    x1 = x + _dot(hml_ref[...], wo_ref[0, :ML_WIDTH, :]) + _dot(hsb_ref[...], wo_ref[0, ML_WIDTH:, :])
    q = _dot(_rms(x1, gmem_ref[0]).astype(BF16), wq_ref[0]).astype(BF16)
    kk = k_ref[0, 0]
    vv = v_ref[0, 0]
    dh = q.shape[1] // XA_HEADS
    heads = [slice(hd * dh, (hd + 1) * dh) for hd in range(XA_HEADS)]
    scores = [_dot_nt(q[:, sl], kk[:, sl]) * (dh ** -0.5) for sl in heads]
    probs = []
    for s in scores:
        p = jnp.exp(s - jnp.max(s, axis=-1, keepdims=True))
        probs.append((p / jnp.sum(p, axis=-1, keepdims=True)).astype(BF16))
    outs = [_dot(p, vv[:, sl]).astype(BF16) for p, sl in zip(probs, heads)]
    x2 = x1 + _dot(jnp.concatenate(outs, axis=-1), wmo_ref[0])
    x2_ref[...] = x2
    h3_hi, h3_lo = _split(_rms(x2, gffn_ref[0]))
    h3_ref[...] = h3_hi
    wrh = wrh_ref[0]
    both = _dot(h3_hi, jnp.concatenate([wrh, wrl_ref[0]], axis=1))
    lg = both[:, :LANES] + both[:, LANES:] + _dot(h3_lo, wrh) + br_ref[0]
    pack, counts = _route(lg.T, tm)
    rt_ref[...] = pack
    gcol_ref[...] = jnp.concatenate([pack, jnp.zeros((LANES - 8, tm), F32)], axis=0).T
    cnt_ref[0] = counts


def _post(l, B, S, x2, y, hml, hsb, w_out, ln_mem, w_mq, kx, vx, w_mo, ln_ffn, wr_hi, wr_lo, b_r):
    N, D = x2.shape
    M = kx.shape[2]
    tm = TOKEN_TILE
    tpb = S // tm
    row = lambda i: (i, 0)
    lay = lambda i: (l, 0, 0)
    ins = [x2] + ([y] if y is not None else [])
    in_specs = [pl.BlockSpec((tm, D), row)] * len(ins) + [
        pl.BlockSpec((tm, ML_WIDTH), row),
        pl.BlockSpec((tm, SB_WIDTH), row),
        pl.BlockSpec((1, D, D), lay),
        pl.BlockSpec((1, 1, D), lay),
        pl.BlockSpec((1, D, D), lay),
        pl.BlockSpec((1, 1, M, D), lambda i: (l, i // tpb, 0, 0)),
        pl.BlockSpec((1, 1, M, D), lambda i: (l, i // tpb, 0, 0)),
        pl.BlockSpec((1, D, D), lay),
        pl.BlockSpec((1, 1, D), lay),
        pl.BlockSpec((1, D, LANES), lay),
        pl.BlockSpec((1, D, LANES), lay),
        pl.BlockSpec((1, 1, LANES), lay),
    ]
    return pl.pallas_call(
        functools.partial(_post_kernel, y is not None),
        grid=(N // tm,),
        in_specs=in_specs,
        out_specs=[
            pl.BlockSpec((tm, D), row),
            pl.BlockSpec((tm, D), row),
            pl.BlockSpec((8, tm), lambda i: (0, i)),
            pl.BlockSpec((tm, LANES), row),
            pl.BlockSpec((1, 8, LANES), lambda i: (i, 0, 0)),
        ],
        out_shape=[
            jax.ShapeDtypeStruct((N, D), F32),
            jax.ShapeDtypeStruct((N, D), BF16),
            jax.ShapeDtypeStruct((8, N), F32),
            jax.ShapeDtypeStruct((N, LANES), F32),
            jax.ShapeDtypeStruct((N // tm, 8, LANES), I32),
        ],
        compiler_params=_params(("arbitrary",)),
        name="post",
    )(*ins, hml, hsb, w_out, ln_mem, w_mq, kx, vx, w_mo, ln_ffn, wr_hi, wr_lo, b_r)


def _moe_kernel(T, cnt_ref, h3_ref, rt_ref, gcol_ref, w1_ref, w3_ref, w2_ref, y_ref,
                xs_ref, gs_ref, ys_ref, sl_ref, gl_ref):
    G, CH, CH_TAIL, SBK, WIN, RT = N_GROUPS, MOE_CHUNK, MOE_CHUNK_TAIL, SORT_BLOCK, SORT_WIN, ROW_TILE_BF16
    b, g, hf = pl.program_id(0), pl.program_id(1), pl.program_id(2)
    nsb = T // SBK
    nbat = sl_ref.shape[0]
    n_half = pl.num_programs(2)
    D = h3_ref.shape[1]

    pc = [[((cnt_ref[(b * nsb + j) * G + gg] + (RT - 1)) // RT) * RT for j in range(nsb)] for gg in range(G)]
    base = [0]
    for gg in range(G):
        base.append(base[-1] + functools.reduce(lambda u, w: u + w, pc[gg]))

    def seg_starts(j):
        out, src = [], 0
        for gg in range(G):
            dst = base[gg]
            for jp in range(j):
                dst = dst + pc[gg][jp]
            out.append((src, dst))
            src = src + pc[gg][j]
        return out

    @pl.when((g == 0) & (hf == 0))
    def _sort():
        total = pl.multiple_of(base[G], RT)
        xs_ref[pl.ds(total, CH), :] = jnp.zeros((CH, D), BF16)
        gs_ref[pl.ds(total, CH), :] = jnp.zeros((CH, LANES), F32)
        riota = lax.broadcasted_iota(I32, (WIN, SBK), 0)
        for j0 in range(0, nsb, nbat):
            for jj in range(nbat):
                j = j0 + jj
                lp = rt_ref[5:6, j * SBK:(j + 1) * SBK].astype(I32)
                perm = jnp.where(riota == lp, 1.0, 0.0).astype(BF16)
                sl_ref[jj] = _dot(perm, h3_ref[j * SBK:(j + 1) * SBK, :]).astype(BF16)
                g_hi, g_lo = _split(gcol_ref[j * SBK:(j + 1) * SBK, :])
                both = _dot(perm, jnp.concatenate([g_hi, g_lo], axis=1))
                gl_ref[jj] = both[:, :LANES] + both[:, LANES:]
            for jj in range(nbat):
                for gg, (src, dst) in enumerate(seg_starts(j0 + jj)):
                    def copy_in(c, carry, jj=jj, src=src, dst=dst):
                        s = pl.multiple_of(src + c * RT, RT)
                        t = pl.multiple_of(dst + c * RT, RT)
                        xs_ref[pl.ds(t, RT), :] = sl_ref[jj, pl.ds(s, RT), :]
                        gs_ref[pl.ds(t, RT), :] = gl_ref[jj, pl.ds(s, RT), :]
                        return carry
                    lax.fori_loop(0, pc[gg][j0 + jj] // RT, copy_in, 0)

    rows_g = base[1] - base[0]
    base_g = base[0]
    for gg in range(1, G):
        rows_g = jnp.where(g == gg, base[gg + 1] - base[gg], rows_g)
        base_g = jnp.where(g == gg, base[gg], base_g)

    def run_chunk(r0, rows):
        r0 = pl.multiple_of(r0, RT)
        xc = xs_ref[pl.ds(r0, rows), :]
        gc = gs_ref[pl.ds(r0, rows), :]
        ups = [(_dot(xc, w1_ref[e]), _dot(xc, w3_ref[e])) for e in range(EXPERTS_PER_STEP)]
        hms = []
        for e, (a, u) in enumerate(ups):
            gate = gc[:, e:e + 1]
            for k in range(1, EXP_PER_GROUP // EXPERTS_PER_STEP):
                gate = jnp.where(hf == k, gc[:, k * EXPERTS_PER_STEP + e:k * EXPERTS_PER_STEP + e + 1], gate)
            hms.append(((a * _sigmoid(a)) * u * gate).astype(BF16))
        acc = functools.reduce(jnp.add, [_dot(hm, w2_ref[e]) for e, hm in enumerate(hms)])

        @pl.when(hf == 0)
        def _():
            ys_ref[pl.ds(r0, rows), :] = acc

        @pl.when(hf != 0)
        def _():
            ys_ref[pl.ds(r0, rows), :] += acc

    big = 2 * CH
    n_big = rows_g // big
    rest = rows_g - n_big * big
    merged = jnp.where((n_big >= 1) & (rest > 0) & (rest <= CH_TAIL), 1, 0)
    start, left = base_g, rows_g
    for size, n in ((big, n_big - merged), (big + CH_TAIL, merged), (CH, None)):
        if n is None:
            n = left // size

        def body(c, carry, start=start, size=size):
            run_chunk(start + c * size, size)
            return carry

        lax.fori_loop(0, n, body, 0)
        start, left = start + n * size, jnp.maximum(left - n * size, 0)

    def tail_body(c, carry, start=start):
        run_chunk(start + c * CH_TAIL, CH_TAIL)
        return carry

    lax.fori_loop(0, (left + (CH_TAIL - 1)) // CH_TAIL, tail_body, 0)

    @pl.when((g == G - 1) & (hf == n_half - 1))
    def _unsort():
        ciota = lax.broadcasted_iota(I32, (SBK, WIN), 1)
        for j0 in range(0, nsb, nbat):
            for jj in range(nbat):
                for gg, (src, dst) in enumerate(seg_starts(j0 + jj)):
                    def copy_out(c, carry, jj=jj, src=src, dst=dst):
                        s = pl.multiple_of(src + c * RT, RT)
                        t = pl.multiple_of(dst + c * RT, RT)
                        sl_ref[jj, pl.ds(s, RT), :] = ys_ref[pl.ds(t, RT), :].astype(BF16)
                        return carry
                    lax.fori_loop(0, pc[gg][j0 + jj] // RT, copy_out, 0)
            for jj in range(nbat):
                j = j0 + jj
                lp = gcol_ref[j * SBK:(j + 1) * SBK, 5:6].astype(I32)
                perm_t = jnp.where(ciota == lp, 1.0, 0.0).astype(BF16)
                y_ref[j * SBK:(j + 1) * SBK, :] = _dot(perm_t, sl_ref[jj]).astype(BF16)


def _moe(l, T, counts, h3, rt, gcol, w1, w3, w2):
    N, D = h3.shape
    G, E, EPS_ = N_GROUPS, EXP_PER_GROUP, EXPERTS_PER_STEP
    F = w1.shape[-1]
    rows = T + (T // SORT_BLOCK) * G * (ROW_TILE_BF16 - 1) + MOE_CHUNK
    rows = -(-rows // ROW_TILE_BF16) * ROW_TILE_BF16
    nsb = T // SORT_BLOCK
    nbat = next(n for n in (SORT_BATCH, 2, 1) if nsb % n == 0)
    grid_spec = pltpu.PrefetchScalarGridSpec(
        num_scalar_prefetch=1,
        grid=(N // T, G, E // EPS_),
        in_specs=[
            pl.BlockSpec((T, D), lambda b, g, hf, cnt: (b, 0)),
            pl.BlockSpec((8, T), lambda b, g, hf, cnt: (0, b)),
            pl.BlockSpec((T, LANES), lambda b, g, hf, cnt: (b, 0)),
            pl.BlockSpec((None, None, EPS_, D, F), lambda b, g, hf, cnt: (l, g, hf, 0, 0)),
            pl.BlockSpec((None, None, EPS_, D, F), lambda b, g, hf, cnt: (l, g, hf, 0, 0)),
            pl.BlockSpec((None, None, EPS_, F, D), lambda b, g, hf, cnt: (l, g, hf, 0, 0)),
        ],
        out_specs=pl.BlockSpec((T, D), lambda b, g, hf, cnt: (b, 0)),
        scratch_shapes=[
            pltpu.VMEM((rows, D), BF16),
            pltpu.VMEM((rows, LANES), F32),
            pltpu.VMEM((rows, D), F32),
            pltpu.VMEM((nbat, SORT_WIN, D), BF16),
            pltpu.VMEM((nbat, SORT_WIN, LANES), F32),
        ],
    )
    return pl.pallas_call(
        functools.partial(_moe_kernel, T),
        grid_spec=grid_spec,
        out_shape=jax.ShapeDtypeStruct((N, D), BF16),
        compiler_params=_params(("arbitrary", "arbitrary", "arbitrary")),
        name="moe",
    )(counts, h3, rt, gcol, w1, w3, w2)


def _final_kernel(x_ref, y_ref, g_ref, o_ref):
    o_ref[...] = _rms(x_ref[...] + y_ref[...].astype(F32), g_ref[...])


def _final(x2, y, ln_final):
    N, D = x2.shape
    tm = TOKEN_TILE
    return pl.pallas_call(
        _final_kernel,
        grid=(N // tm,),
        in_specs=[pl.BlockSpec((tm, D), lambda i: (i, 0)), pl.BlockSpec((tm, D), lambda i: (i, 0)),
                  pl.BlockSpec((1, D), lambda i: (0, 0))],
        out_specs=pl.BlockSpec((tm, D), lambda i: (i, 0)),
        out_shape=jax.ShapeDtypeStruct((N, D), F32),
        compiler_params=_params(("arbitrary",)),
        name="final_norm",
    )(x2, y, ln_final.reshape(1, D))


def kernel(x, mem, ln_mix, w_in, conv_w, conv_b, i_bias, f_bias, ml_head_g, sb_head_g, w_out, ln_mem, ln_memkv,
           w_mq, w_mk, w_mv, w_mo, ln_ffn, w_rg, b_rg, w_re, b_re, w_e1, w_e3, w_e2, ln_final):
    B, S, D = x.shape
    N = B * S
    depth = w_in.shape[0]
    G, E = N_GROUPS, EXP_PER_GROUP
    assert S % TOKEN_TILE == 0 and TOKEN_TILE % ML_CHUNK == 0 and TOKEN_TILE % SORT_BLOCK == 0 and S % SB_SPAN == 0
    T = 2048 if S % 2048 == 0 else S

    c0, c1 = 4 * ML_WIDTH, 4 * ML_WIDTH + 2 * ML_HEADS
    w_ml = w_in[:, :, :c0].astype(BF16)
    w_sqt = jnp.swapaxes(w_in[:, :, c1:c1 + SB_WIDTH], 1, 2).astype(BF16)
    w_sk = w_in[:, :, c1 + SB_WIDTH:c1 + 2 * SB_WIDTH].astype(BF16)
    w_svt = jnp.swapaxes(w_in[:, :, c1 + 2 * SB_WIDTH:], 1, 2).astype(BF16)
    w_gate = w_in[:, :, c0:c1]
    w_g = jnp.pad(w_gate, ((0, 0), (0, 0), (0, LANES - 8))).astype(BF16)
    gate_b = jnp.concatenate([i_bias, f_bias], axis=-1)
    b_g = jnp.pad(gate_b, ((0, 0), (0, LANES - 8))).reshape(depth, 1, LANES)
    w_r = jnp.pad(jnp.concatenate([w_rg, w_re], axis=-1), ((0, 0), (0, 0), (0, LANES - G - G * E)))
    wr_hi = w_r.astype(BF16)
    wr_lo = (w_r - wr_hi.astype(F32)).astype(BF16)
    b_r = jnp.pad(jnp.concatenate([b_rg, b_re], axis=-1), ((0, 0), (0, LANES - G - G * E))).reshape(depth, 1, LANES)
    w_out_b, w_mq_b, w_mo_b = w_out.astype(BF16), w_mq.astype(BF16), w_mo.astype(BF16)
    w1, w3, w2 = w_e1.astype(BF16), w_e3.astype(BF16), w_e2.astype(BF16)
    r3 = lambda a: a.reshape(depth, 1, a.shape[-1])

    kx, vx = _memkv(mem, ln_memkv, w_mk.astype(BF16), w_mv.astype(BF16))

    x2, y = x.reshape(N, D), None
    nsub = TOKEN_TILE // SORT_BLOCK
    for l in range(depth):
        qka, vo, sqt, sk, svt, gc, gt = _proj(l, S, x2, y, r3(ln_mix), w_ml, w_sqt, w_sk, w_svt, w_g, b_g,
                                              conv_w, r3(conv_b))
        hml = _mlstm(l, B, S, qka, vo, gc, gt, r3(ml_head_g))
        hsb = _stickbreak(l, B, S, sqt, sk, svt, r3(sb_head_g))
        x2, h3, rt, gcol, cnt = _post(l, B, S, x2, y, hml, hsb, w_out_b, r3(ln_mem), w_mq_b, kx, vx, w_mo_b,
                                      r3(ln_ffn), wr_hi, wr_lo, b_r)
        counts = jnp.swapaxes(cnt[:, :G, :nsub], 1, 2).reshape(-1)
        y = _moe(l, T, counts, h3, rt, gcol, w1, w3, w2)
    return _final(x2, y, ln_final).reshape(B, S, D)
```
